```python
import jax, jax.numpy as jnp
from jax import lax
import numpy as np

D_MODEL = 1024
BATCH = 4
SEQ = 4096
DEPTH = 1
DEC_BATCH = 128
DEC_SEQ = 4
PAST_LEN = 16384
PAGE_SIZE = 128

N_HEADS = 8
QK_NOPE = 64
QK_ROPE = 32
V_DIM = 64
Q_LORA = 384
KV_LORA = 256
CONV_DIM = 512
CONV_WIDTH = 31
D_FF = 2816
N_MEM = 256
MEM_HEADS = 4
MEM_HEAD_DIM = D_MODEL // MEM_HEADS

ROPE_BASE = 10000.0
EPS = 1e-6
QUERY_BLOCK = 128
NEG_INF = -1e30
ATTN_SCALE = (QK_NOPE + QK_ROPE) ** -0.5
MEM_SCALE = MEM_HEAD_DIM ** -0.5
D_IN = Q_LORA + KV_LORA + QK_ROPE + 2 * CONV_DIM
D_MIX = N_HEADS * V_DIM + CONV_DIM

kernel_name = 'hymba_mla_conformer_macaron_decoder_step'


def rmsnorm(x, g):
    xf = x.astype(jnp.float32)
    y = xf * lax.rsqrt(jnp.mean(xf * xf, axis=-1, keepdims=True) + EPS)
    return (y * g.astype(jnp.float32)).astype(x.dtype)


def layernorm(x, g, b):
    xf = x.astype(jnp.float32)
    mu = jnp.mean(xf, axis=-1, keepdims=True)
    var = jnp.mean(jnp.square(xf - mu), axis=-1, keepdims=True)
    y = (xf - mu) * lax.rsqrt(var + EPS) * g.astype(jnp.float32) + b.astype(jnp.float32)
    return y.astype(x.dtype)


def rope(x, pos):
    half = QK_ROPE // 2
    inv = ROPE_BASE ** (-jnp.arange(half, dtype=jnp.float32) / half)
    ang = pos.astype(jnp.float32)[:, None] * inv[None, :]
    shp = (pos.shape[0],) + (1,) * (x.ndim - 3) + (half,)
    cos = jnp.cos(ang).reshape(shp)
    sin = jnp.sin(ang).reshape(shp)
    xf = x.astype(jnp.float32)
    x1, x2 = xf[..., :half], xf[..., half:]
    return jnp.concatenate([x1 * cos - x2 * sin, x2 * cos + x1 * sin], axis=-1).astype(x.dtype)


def swiglu_ffn(x, g, w_gate, w_up, w_down):
    h = rmsnorm(x, g)
    return (jax.nn.silu(h @ w_gate) * (h @ w_up)) @ w_down


def mixer_project(x, pos, mix_norm, w_in, q_norm, w_q_up, kv_norm, w_k_up):
    B, T = x.shape[:2]
    z = rmsnorm(x, mix_norm) @ w_in
    cq, ckv, kr, uc = jnp.split(z, [Q_LORA, Q_LORA + KV_LORA, Q_LORA + KV_LORA + QK_ROPE], axis=-1)
    q = (rmsnorm(cq, q_norm) @ w_q_up).reshape(B, T, N_HEADS, QK_NOPE + QK_ROPE)
    q_nope, q_rope = q[..., :QK_NOPE], rope(q[..., QK_NOPE:], pos)
    q_lat = jnp.einsum('bthd,chd->bthc', q_nope, w_k_up)
    c_kv = rmsnorm(ckv, kv_norm)
    k_rope = rope(kr, pos)
    a, gt = jnp.split(uc, 2, axis=-1)
    u = a * jax.nn.sigmoid(gt)
    return q_lat, q_rope, c_kv, k_rope, u


def mla_attend(q_lat, q_rope, q_pos, segs):
    B, Q = q_lat.shape[:2]
    blk = QUERY_BLOCK if Q % QUERY_BLOCK == 0 else Q
    nb = Q // blk

    def to_blocks(a):
        return jnp.moveaxis(a.reshape((B, nb, blk) + a.shape[2:]), 1, 0)

    def one_block(args):
        ql, qr, qp = args
        scores = []
        for c, kr, kp in segs:
            s = (jnp.einsum('bqhc,btc->bhqt', ql, c)
                 + jnp.einsum('bqhr,btr->bhqt', qr, kr)).astype(jnp.float32) * ATTN_SCALE
            scores.append(jnp.where(kp[None, :] <= qp[:, None], s, NEG_INF))
        p = jax.nn.softmax(jnp.concatenate(scores, axis=-1), axis=-1).astype(ql.dtype)
        out = None
        off = 0
        for c, _, kp in segs:
            n = kp.shape[0]
            part = jnp.einsum('bhqt,btc->bqhc', p[..., off:off + n], c)
            out = part if out is None else out + part
            off += n
        return out

    o = lax.map(one_block, (to_blocks(q_lat), to_blocks(q_rope), q_pos.reshape(nb, blk)))
    return jnp.moveaxis(o, 0, 1).reshape(B, Q, N_HEADS, KV_LORA)


def conv_module(u_ext, conv_w, conv_b, ln_g, ln_b):
    y = lax.conv_general_dilated(u_ext, conv_w[:, None, :].astype(u_ext.dtype), window_strides=(1,),
                                 padding='VALID', dimension_numbers=('NWC', 'WIO', 'NWC'),
                                 feature_group_count=CONV_DIM) + conv_b
    return jax.nn.silu(layernorm(y, ln_g, ln_b))


def mix_output(o_lat, u_ext, w_v_up, conv_w, conv_b, ln_g, ln_b, w_out):
    B, T = o_lat.shape[:2]
    mla_o = jnp.einsum('bqhc,chv->bqhv', o_lat, w_v_up).reshape(B, T, N_HEADS * V_DIM)
    conv_o = conv_module(u_ext, conv_w, conv_b, ln_g, ln_b)
    return jnp.concatenate([mla_o, conv_o], axis=-1) @ w_out


def memory_kv(mem, mem_norm, w_mk, w_mv):
    B = mem.shape[0]
    mn = rmsnorm(mem, mem_norm)
    mk = (mn @ w_mk).reshape(B, N_MEM, MEM_HEADS, MEM_HEAD_DIM)
    mv = (mn @ w_mv).reshape(B, N_MEM, MEM_HEADS, MEM_HEAD_DIM)
    return mk, mv


def cross_attend(x, cross_norm, w_cq, mk, mv, w_co):
    B, T = x.shape[:2]
    q = (rmsnorm(x, cross_norm) @ w_cq).reshape(B, T, MEM_HEADS, MEM_HEAD_DIM)
    s = jnp.einsum('bthd,bmhd->bhtm', q, mk.astype(q.dtype)).astype(jnp.float32) * MEM_SCALE
    p = jax.nn.softmax(s, axis=-1).astype(q.dtype)
    o = jnp.einsum('bhtm,bmhd->bthd', p, mv.astype(q.dtype)).reshape(B, T, D_MODEL)
    return o @ w_co


def setup_inputs(seed: int = 0) -> dict:
    key = jax.random.key(seed)
    keys = iter(jax.random.split(key, 48))
    f32 = jnp.float32

    def nrm(shape, scale=1.0):
        return jax.random.normal(next(keys), shape, f32) * scale

    def gain(n):
        return 1.0 + nrm((DEPTH, n), 0.01)

    n_pages = PAST_LEN // PAGE_SIZE
    n_used = DEC_BATCH * n_pages
    n_phys = (n_used * 5) // 4
    perm = jax.random.permutation(next(keys), n_phys)
    page_table = perm[:n_used].reshape(DEC_BATCH, n_pages).astype(jnp.int32)

    inp = {}
    inp['x_prompt'] = nrm((BATCH, SEQ, D_MODEL))
    inp['x_sample'] = nrm((DEC_BATCH, DEC_SEQ, D_MODEL))
    inp['mem_prompt'] = nrm((BATCH, N_MEM, D_MODEL))
    inp['cache_kv_latent'] = nrm((DEPTH, n_phys, PAGE_SIZE, KV_LORA))
    inp['cache_k_rope'] = nrm((DEPTH, n_phys, PAGE_SIZE, QK_ROPE))
    inp['state_conv'] = nrm((DEPTH, DEC_BATCH, CONV_WIDTH - 1, CONV_DIM), 0.5)
    inp['cache_mem_k'] = nrm((DEPTH, DEC_BATCH, N_MEM, MEM_HEADS, MEM_HEAD_DIM))
    inp['cache_mem_v'] = nrm((DEPTH, DEC_BATCH, N_MEM, MEM_HEADS, MEM_HEAD_DIM))
    inp['page_table'] = page_table
    inp['ffn1_norm'] = gain(D_MODEL)
    inp['ffn1_w_gate'] = nrm((DEPTH, D_MODEL, D_FF), D_MODEL ** -0.5)
    inp['ffn1_w_up'] = nrm((DEPTH, D_MODEL, D_FF), D_MODEL ** -0.5)
    inp['ffn1_w_down'] = nrm((DEPTH, D_FF, D_MODEL), D_FF ** -0.5)
    inp['mix_norm'] = gain(D_MODEL)
    inp['w_in'] = nrm((DEPTH, D_MODEL, D_IN), D_MODEL ** -0.5)
    inp['q_norm'] = gain(Q_LORA)
    inp['w_q_up'] = nrm((DEPTH, Q_LORA, N_HEADS * (QK_NOPE + QK_ROPE)), Q_LORA ** -0.5)
    inp['kv_norm'] = gain(KV_LORA)
    inp['w_k_up'] = nrm((DEPTH, KV_LORA, N_HEADS, QK_NOPE), KV_LORA ** -0.5)
    inp['w_v_up'] = nrm((DEPTH, KV_LORA, N_HEADS, V_DIM), KV_LORA ** -0.5)
    inp['conv_w'] = nrm((DEPTH, CONV_WIDTH, CONV_DIM), CONV_WIDTH ** -0.5)
    inp['conv_b'] = nrm((DEPTH, CONV_DIM), 0.01)
    inp['conv_ln_g'] = gain(CONV_DIM)
    inp['conv_ln_b'] = nrm((DEPTH, CONV_DIM), 0.01)
    inp['w_out'] = nrm((DEPTH, D_MIX, D_MODEL), D_MIX ** -0.5)
    inp['cross_norm'] = gain(D_MODEL)
    inp['mem_norm'] = gain(D_MODEL)
    inp['w_cq'] = nrm((DEPTH, D_MODEL, D_MODEL), D_MODEL ** -0.5)
    inp['w_mk'] = nrm((DEPTH, D_MODEL, D_MODEL), D_MODEL ** -0.5)
    inp['w_mv'] = nrm((DEPTH, D_MODEL, D_MODEL), D_MODEL ** -0.5)
    inp['w_co'] = nrm((DEPTH, D_MODEL, D_MODEL), D_MODEL ** -0.5)
    inp['ffn2_norm'] = gain(D_MODEL)
    inp['ffn2_w_gate'] = nrm((DEPTH, D_MODEL, D_FF), D_MODEL ** -0.5)
    inp['ffn2_w_up'] = nrm((DEPTH, D_MODEL, D_FF), D_MODEL ** -0.5)
    inp['ffn2_w_down'] = nrm((DEPTH, D_FF, D_MODEL), D_FF ** -0.5)
    inp['final_norm'] = 1.0 + nrm((D_MODEL,), 0.01)
    return inp


def reference(x_prompt, x_sample, mem_prompt, cache_kv_latent, cache_k_rope, state_conv,
              cache_mem_k, cache_mem_v, page_table,
              ffn1_norm, ffn1_w_gate, ffn1_w_up, ffn1_w_down,
              mix_norm, w_in, q_norm, w_q_up, kv_norm, w_k_up, w_v_up,
              conv_w, conv_b, conv_ln_g, conv_ln_b, w_out,
              cross_norm, mem_norm, w_cq, w_mk, w_mv, w_co,
              ffn2_norm, ffn2_w_gate, ffn2_w_up, ffn2_w_down, final_norm):
    xp, xs = x_prompt, x_sample
    T = xp.shape[1]
    Bd, Td = xs.shape[:2]
    pos_p = jnp.arange(T, dtype=jnp.int32)
    pos_s = PAST_LEN + jnp.arange(Td, dtype=jnp.int32)
    pos_past = jnp.arange(PAST_LEN, dtype=jnp.int32)

    kvl_p, kr_p, cs_p, mk_p, mv_p = [], [], [], [], []
    kvl_s, kr_s, cs_s = [], [], []
    for l in range(DEPTH):
        xp = xp + 0.5 * swiglu_ffn(xp, ffn1_norm[l], ffn1_w_gate[l], ffn1_w_up[l], ffn1_w_down[l])
        xs = xs + 0.5 * swiglu_ffn(xs, ffn1_norm[l], ffn1_w_gate[l], ffn1_w_up[l], ffn1_w_down[l])

        q_lat, q_rope, c_kv, k_rope, u = mixer_project(
            xp, pos_p, mix_norm[l], w_in[l], q_norm[l], w_q_up[l], kv_norm[l], w_k_up[l])
        o_lat = mla_attend(q_lat, q_rope, pos_p, ((c_kv, k_rope, pos_p),))
        u_ext = jnp.pad(u, ((0, 0), (CONV_WIDTH - 1, 0), (0, 0)))
        xp = xp + mix_output(o_lat, u_ext, w_v_up[l], conv_w[l], conv_b[l],
                             conv_ln_g[l], conv_ln_b[l], w_out[l])
        kvl_p.append(c_kv)
        kr_p.append(k_rope)
        cs_p.append(u_ext[:, -(CONV_WIDTH - 1):])

        q_lat, q_rope, c_new, kr_new, u = mixer_project(
            xs, pos_s, mix_norm[l], w_in[l], q_norm[l], w_q_up[l], kv_norm[l], w_k_up[l])
        past_c = cache_kv_latent[l][page_table].reshape(Bd, PAST_LEN, KV_LORA).astype(c_new.dtype)
        past_kr = cache_k_rope[l][page_table].reshape(Bd, PAST_LEN, QK_ROPE).astype(kr_new.dtype)
        o_lat = mla_attend(q_lat, q_rope, pos_s,
                           ((past_c, past_kr, pos_past), (c_new, kr_new, pos_s)))
        u_ext = jnp.concatenate([state_conv[l].astype(u.dtype), u], axis=1)
        xs = xs + mix_output(o_lat, u_ext, w_v_up[l], conv_w[l], conv_b[l],
                             conv_ln_g[l], conv_ln_b[l], w_out[l])
        kvl_s.append(c_new)
        kr_s.append(kr_new)
        cs_s.append(u_ext[:, -(CONV_WIDTH - 1):])

        mk, mv = memory_kv(mem_prompt, mem_norm[l], w_mk[l], w_mv[l])
        xp = xp + cross_attend(xp, cross_norm[l], w_cq[l], mk, mv, w_co[l])
        xs = xs + cross_attend(xs, cross_norm[l], w_cq[l], cache_mem_k[l], cache_mem_v[l], w_co[l])
        mk_p.append(mk)
        mv_p.append(mv)

        xp = xp + 0.5 * swiglu_ffn(xp, ffn2_norm[l], ffn2_w_gate[l], ffn2_w_up[l], ffn2_w_down[l])
        xs = xs + 0.5 * swiglu_ffn(xs, ffn2_norm[l], ffn2_w_gate[l], ffn2_w_up[l], ffn2_w_down[l])

    y_prompt = rmsnorm(xp, final_norm)
    y_sample = rmsnorm(xs, final_norm)
    return (y_prompt, y_sample,
            jnp.stack(kvl_p), jnp.stack(kr_p), jnp.stack(cs_p), jnp.stack(mk_p), jnp.stack(mv_p),
            jnp.stack(kvl_s), jnp.stack(kr_s), jnp.stack(cs_s))
```

```python
import functools

import jax
import jax.numpy as jnp
from jax import lax
from jax.experimental import pallas as pl
from jax.experimental.pallas import tpu as pltpu

F32 = jnp.float32
BF16 = jnp.bfloat16

D_MODEL = 1024
N_HEADS = 8
QK_NOPE = 64
QK_ROPE = 32
V_DIM = 64
Q_LORA = 384
KV_LORA = 256
CONV_DIM = 512
CONV_WIDTH = 31
D_FF = 2816
N_MEM = 256
MEM_HEADS = 4
MEM_HEAD_DIM = D_MODEL // MEM_HEADS
PAGE_SIZE = 128
ROPE_BASE = 10000.0
EPS = 1e-6
NEG_INF = -1e30
ATTN_SCALE = (QK_NOPE + QK_ROPE) ** -0.5
MEM_SCALE = MEM_HEAD_DIM ** -0.5

VMEM_LIMIT = 56 * 1024 * 1024

TM = 512
NOPE_PAD = 128
ROPE_ALL = N_HEADS * QK_ROPE
KR_TILE = 128
K_FULL = KV_LORA + KR_TILE
O_CQ, O_CKV, O_A, O_GT, O_KR, O_KRS, W_IN_COLS = 0, 384, 640, 1152, 1664, 1792, 1920
O_QN, O_QR, O_QRS, W_Q_COLS = 0, 1024, 1280, 1536

BQ = 128
BK = 512
PAGES_PER_STEP = 16
CONV_TT = 256
CONV_R = 32
CONV_HALO = 32


def _const_spec(shape):
    nd = len(shape)
    return pl.BlockSpec(shape, lambda *_: (0,) * nd, pipeline_mode=pl.Buffered(1))


def _params(sem):
    return pltpu.CompilerParams(dimension_semantics=sem, vmem_limit_bytes=VMEM_LIMIT)


def _rms(x, g):
    ms = jnp.mean(x * x, axis=-1, keepdims=True)
    return x * lax.rsqrt(ms + EPS) * g


def _swiglu(h, wg_ref, wu_ref, wd_ref):
    gate = jnp.dot(h, wg_ref[...], preferred_element_type=F32)
    up = jnp.dot(h, wu_ref[...], preferred_element_type=F32)
    a = (gate * jax.nn.sigmoid(gate) * up).astype(BF16)
    return jnp.dot(a, wd_ref[...], preferred_element_type=F32)


def _ffn_body(x_ref, g_ref, wg_ref, wu_ref, wd_ref, o_ref):
    x = x_ref[...]
    h = _rms(x, g_ref[...]).astype(BF16)
    o_ref[...] = x + 0.5 * _swiglu(h, wg_ref, wu_ref, wd_ref)


def _ffn(x, g, wg, wu, wd):
    n = x.shape[0]
    row = pl.BlockSpec((TM, D_MODEL), lambda i: (i, 0))
    return pl.pallas_call(
        _ffn_body,
        grid=(n // TM,),
        in_specs=[row, _const_spec((1, D_MODEL)), _const_spec((D_MODEL, D_FF)),
                  _const_spec((D_MODEL, D_FF)), _const_spec((D_FF, D_MODEL))],
        out_specs=row,
        out_shape=jax.ShapeDtypeStruct((n, D_MODEL), F32),
        compiler_params=_params(("parallel",)),
        name="ffn1",
    )(x, g, wg, wu, wd)


def _proj_body(x_ref, cos_ref, sin_ref, g_ref, win_ref, qn_ref, wq_ref, kvn_ref, wk_ref,
               qlat_ref, qrope_ref, kfull_ref, ckv_ref, krope_ref, u_ref):
    h = _rms(x_ref[...], g_ref[...]).astype(BF16)
    z = jnp.dot(h, win_ref[...], preferred_element_type=F32)
    cos = cos_ref[...]
    sin = sin_ref[...]
    u_ref[...] = z[:, O_A:O_GT] * jax.nn.sigmoid(z[:, O_GT:O_KR])
    c = _rms(z[:, O_CKV:O_A], kvn_ref[...])
    ckv_ref[...] = c
    kr = z[:, O_KR:O_KRS] * cos[:, :KR_TILE] + z[:, O_KRS:W_IN_COLS] * sin[:, :KR_TILE]
    krope_ref[...] = kr[:, :QK_ROPE]
    kfull_ref[:, 0:KV_LORA] = c.astype(BF16)
    kfull_ref[:, KV_LORA:K_FULL] = kr.astype(BF16)
    cqn = _rms(z[:, O_CQ:O_CKV], qn_ref[...]).astype(BF16)
    q = jnp.dot(cqn, wq_ref[...], preferred_element_type=F32)
    qrope_ref[...] = (q[:, O_QR:O_QRS] * cos + q[:, O_QRS:W_Q_COLS] * sin).astype(BF16)
    for hh in range(N_HEADS):
        qn = q[:, hh * NOPE_PAD:(hh + 1) * NOPE_PAD].astype(BF16)
        qlat_ref[:, hh * KV_LORA:(hh + 1) * KV_LORA] = jnp.dot(
            qn, wk_ref[hh], preferred_element_type=F32).astype(BF16)


def _proj(x, cos, sin, tab_blocks, g, win, qn, wq, kvn, wk):
    n = x.shape[0]
    row = lambda w: pl.BlockSpec((TM, w), lambda i: (i, 0))
    tab = pl.BlockSpec((TM, ROPE_ALL), lambda i: (i % tab_blocks, 0))
    return pl.pallas_call(
        _proj_body,
        grid=(n // TM,),
        in_specs=[row(D_MODEL), tab, tab, _const_spec((1, D_MODEL)),
                  _const_spec((D_MODEL, W_IN_COLS)), _const_spec((1, Q_LORA)),
                  _const_spec((Q_LORA, W_Q_COLS)), _const_spec((1, KV_LORA)),
                  _const_spec((N_HEADS, NOPE_PAD, KV_LORA))],
        out_specs=[row(N_HEADS * KV_LORA), row(ROPE_ALL), row(K_FULL), row(KV_LORA),
                   row(QK_ROPE), row(CONV_DIM)],
        out_shape=[jax.ShapeDtypeStruct((n, N_HEADS * KV_LORA), BF16),
                   jax.ShapeDtypeStruct((n, ROPE_ALL), BF16),
                   jax.ShapeDtypeStruct((n, K_FULL), BF16),
                   jax.ShapeDtypeStruct((n, KV_LORA), F32),
                   jax.ShapeDtypeStruct((n, QK_ROPE), F32),
                   jax.ShapeDtypeStruct((n, CONV_DIM), F32)],
        compiler_params=_params(("parallel",)),
        name="mixer_proj",
    )(x, cos, sin, g, win, qn, wq, kvn, wk)


def _lane_tile(v, n):
    return jnp.concatenate([v] * n, axis=1) if n > 1 else v


def _attn_p_body(qlat_ref, qrope_ref, k_ref, o_ref, qs_ref, m_ref, l_ref, acc_ref):
    i = pl.program_id(1)
    rows = N_HEADS * BQ
    lane = lax.broadcasted_iota(jnp.int32, (BQ, KR_TILE), 1)
    heads_per_tile = KR_TILE // QK_ROPE
    for hh in range(N_HEADS):
        r0 = hh * BQ
        qs_ref[r0:r0 + BQ, 0:KV_LORA] = qlat_ref[:, hh * KV_LORA:(hh + 1) * KV_LORA]
        t = hh // heads_per_tile
        grp = qrope_ref[:, t * KR_TILE:(t + 1) * KR_TILE]
        lo = (hh % heads_per_tile) * QK_ROPE
        keep = jnp.logical_and(lane >= lo, lane < lo + QK_ROPE)
        qs_ref[r0:r0 + BQ, KV_LORA:K_FULL] = jnp.where(keep, grp, jnp.zeros_like(grp))
    m_ref[...] = jnp.full(m_ref.shape, NEG_INF, F32)
    l_ref[...] = jnp.zeros(l_ref.shape, F32)
    acc_ref[...] = jnp.zeros(acc_ref.shape, F32)
    q0 = i * BQ
    n_full = q0 // BK

    def step(kb, masked):
        start = pl.multiple_of(kb * BK, BK)
        k = k_ref[pl.ds(start, BK), :]
        s = lax.dot_general(qs_ref[...], k, (((1,), (1,)), ((), ())),
                            preferred_element_type=F32) * ATTN_SCALE
        if masked:
            r = lax.broadcasted_iota(jnp.int32, (rows, BK), 0)
            col = lax.broadcasted_iota(jnp.int32, (rows, BK), 1)
            qpos = q0 + jnp.bitwise_and(r, BQ - 1)
            s = jnp.where(start + col <= qpos, s, NEG_INF)
        m_prev = m_ref[...]
        m_next = jnp.maximum(m_prev, jnp.max(s, axis=1, keepdims=True))
        alpha = jnp.exp(m_prev - m_next)
        p = jnp.exp(s - _lane_tile(m_next, BK // 128))
        l_ref[...] = alpha * l_ref[...] + jnp.sum(p, axis=1, keepdims=True)
        m_ref[...] = m_next
        pv = jnp.dot(p.astype(BF16), k[:, 0:KV_LORA], preferred_element_type=F32)
        acc_ref[...] = acc_ref[...] * _lane_tile(alpha, KV_LORA // 128) + pv

    def full_step(kb, carry):
        step(kb, False)
        return carry

    lax.fori_loop(0, n_full, full_step, 0)
    step(n_full, True)
    inv = 1.0 / l_ref[...]
    for hh in range(N_HEADS):
        r0 = hh * BQ
        o_ref[:, hh * KV_LORA:(hh + 1) * KV_LORA] = (
            acc_ref[r0:r0 + BQ, :] * _lane_tile(inv[r0:r0 + BQ, :], KV_LORA // 128)).astype(BF16)


def _attn_prompt(qlat, qrope, kfull, batch, seq):
    nq = seq // BQ
    rows = N_HEADS * BQ
    return pl.pallas_call(
        _attn_p_body,
        grid=(batch, nq),
        in_specs=[pl.BlockSpec((BQ, N_HEADS * KV_LORA), lambda b, i: (b * nq + i, 0)),
                  pl.BlockSpec((BQ, ROPE_ALL), lambda b, i: (b * nq + i, 0)),
                  pl.BlockSpec((seq, K_FULL), lambda b, i: (b, 0))],
        out_specs=pl.BlockSpec((BQ, N_HEADS * KV_LORA), lambda b, i: (b * nq + i, 0)),
        out_shape=jax.ShapeDtypeStruct((batch * seq, N_HEADS * KV_LORA), BF16),
        scratch_shapes=[pltpu.VMEM((rows, K_FULL), BF16), pltpu.VMEM((rows, 128), F32),
                        pltpu.VMEM((rows, 128), F32), pltpu.VMEM((rows, KV_LORA), F32)],
        compiler_params=_params(("parallel", "arbitrary")),
        name="attn_prompt",
    )(qlat, qrope, kfull)


def _attn_d_body(pt_ref, qlat_ref, qr_ref, cnew_ref, krnew_ref, *rest):
    g_pages = PAGES_PER_STEP
    c_refs = rest[:g_pages]
    kr_refs = rest[g_pages:2 * g_pages]
    o_ref = rest[2 * g_pages]
    kbuf, krbuf, m_ref, l_ref, acc_ref = rest[2 * g_pages + 1:]
    j = pl.program_id(1)
    nj = pl.num_programs(1)
    rows = qlat_ref.shape[0]
    td = rows // N_HEADS

    @pl.when(j == 0)
    def _():
        m_ref[...] = jnp.full(m_ref.shape, NEG_INF, F32)
        l_ref[...] = jnp.zeros(l_ref.shape, F32)
        acc_ref[...] = jnp.zeros(acc_ref.shape, F32)

    for g in range(g_pages):
        kbuf[g * PAGE_SIZE:(g + 1) * PAGE_SIZE, :] = c_refs[g][...].astype(BF16)
        krbuf[g * PAGE_SIZE:(g + 1) * PAGE_SIZE, :] = kr_refs[g][...].astype(BF16)
    q = qlat_ref[...]
    qr = qr_ref[...]
    nt = (((1,), (1,)), ((), ()))
    s = (lax.dot_general(q, kbuf[...], nt, preferred_element_type=F32)
         + lax.dot_general(qr, krbuf[...], nt, preferred_element_type=F32)) * ATTN_SCALE
    n_keys = g_pages * PAGE_SIZE
    m_prev = m_ref[...]
    m_next = jnp.maximum(m_prev, jnp.max(s, axis=1, keepdims=True))
    alpha = jnp.exp(m_prev - m_next)
    p = jnp.exp(s - _lane_tile(m_next, n_keys // 128))
    l_ref[...] = alpha * l_ref[...] + jnp.sum(p, axis=1, keepdims=True)
    m_ref[...] = m_next
    acc_ref[...] = (acc_ref[...] * _lane_tile(alpha, KV_LORA // 128)
                    + jnp.dot(p.astype(BF16), kbuf[...], preferred_element_type=F32))

    @pl.when(j == nj - 1)
    def _():
        qf = q.astype(F32)
        qrf = qr.astype(F32)
        tok = lax.broadcasted_iota(jnp.int32, (rows, 1), 0) // N_HEADS
        s_new = []
        for t2 in range(td):
            cn = cnew_ref[t2:t2 + 1, :].astype(BF16).astype(F32)
            kn = krnew_ref[t2:t2 + 1, :].astype(BF16).astype(F32)
            st = (jnp.sum(qf * cn, axis=1, keepdims=True)
                  + jnp.sum(qrf * kn, axis=1, keepdims=True)) * ATTN_SCALE
            s_new.append(jnp.where(t2 <= tok, st, NEG_INF))
        m_prev = m_ref[...]
        m_cur = s_new[0]
        for st in s_new[1:]:
            m_cur = jnp.maximum(m_cur, st)
        m_next = jnp.maximum(m_prev, m_cur)
        alpha = jnp.exp(m_prev - m_next)
        l_new = alpha * l_ref[...]
        acc = acc_ref[...] * _lane_tile(alpha, KV_LORA // 128)
        for t2 in range(td):
            pt = jnp.exp(s_new[t2] - m_next)
            l_new = l_new + pt
            cn = cnew_ref[t2:t2 + 1, :].astype(BF16).astype(F32)
            acc = acc + _lane_tile(pt.astype(BF16).astype(F32), KV_LORA // 128) * cn
        o_ref[...] = (acc * _lane_tile(1.0 / l_new, KV_LORA // 128)).astype(BF16)


def _attn_decode(page_table, qlat, qrope, cnew, krnew, cache_c, cache_kr, layer):
    bd, rows, _ = qlat.shape
    td = rows // N_HEADS
    n_pages = page_table.shape[1]
    g_pages = PAGES_PER_STEP
    nj = n_pages // g_pages
    pt_flat = page_table.reshape(-1)

    def page_spec(width, g):
        return pl.BlockSpec((None, None, PAGE_SIZE, width),
                            lambda b, j, pt: (layer, pt[b * n_pages + j * g_pages + g], 0, 0))

    per_b = lambda r, w: pl.BlockSpec((None, r, w), lambda b, j, pt: (b, 0, 0))
    in_specs = ([per_b(rows, KV_LORA), per_b(rows, QK_ROPE), per_b(td, KV_LORA), per_b(td, QK_ROPE)]
                + [page_spec(KV_LORA, g) for g in range(g_pages)]
                + [page_spec(QK_ROPE, g) for g in range(g_pages)])
    grid_spec = pltpu.PrefetchScalarGridSpec(
        num_scalar_prefetch=1,
        grid=(bd, nj),
        in_specs=in_specs,
        out_specs=per_b(rows, KV_LORA),
        scratch_shapes=[pltpu.VMEM((g_pages * PAGE_SIZE, KV_LORA), BF16),
                        pltpu.VMEM((g_pages * PAGE_SIZE, QK_ROPE), BF16),
                        pltpu.VMEM((rows, 128), F32), pltpu.VMEM((rows, 128), F32),
                        pltpu.VMEM((rows, KV_LORA), F32)])
    return pl.pallas_call(
        _attn_d_body,
        grid_spec=grid_spec,
        out_shape=jax.ShapeDtypeStruct((bd, rows, KV_LORA), BF16),
        compiler_params=_params(("parallel", "arbitrary")),
        name="attn_decode",
    )(pt_flat, qlat, qrope, cnew, krnew, *([cache_c] * g_pages), *([cache_kr] * g_pages))


def _ln_silu(y, g, b):
    mu = jnp.mean(y, axis=-1, keepdims=True)
    d = y - mu
    var = jnp.mean(d * d, axis=-1, keepdims=True)
    z = d * lax.rsqrt(var + EPS) * g + b
    return z * jax.nn.sigmoid(z)


def _conv_p_body(prev_ref, cur_ref, w_ref, b_ref, g_ref, beta_ref, o_ref, xb_ref):
    i = pl.program_id(1)
    xb_ref[0:CONV_HALO, :] = jnp.where(i > 0, prev_ref[...], 0.0)
    xb_ref[CONV_HALO:CONV_HALO + CONV_TT, :] = cur_ref[...]
    off = CONV_HALO - (CONV_WIDTH - 1)
    for r in range(CONV_TT // CONV_R):
        acc = jnp.zeros((CONV_R // 8, 8, CONV_DIM), F32)
        for k in range(CONV_WIDTH):
            lo = r * CONV_R + off + k
            xs = xb_ref[lo:lo + CONV_R, :].reshape(CONV_R // 8, 8, CONV_DIM)
            acc = acc + xs * w_ref[k][None]
        y = acc.reshape(CONV_R, CONV_DIM) + b_ref[...]
        o_ref[r * CONV_R:(r + 1) * CONV_R, :] = _ln_silu(y, g_ref[...], beta_ref[...]).astype(BF16)


def _conv_prompt(u, w8, b, g, beta, batch, seq):
    nt = seq // CONV_TT
    halo_per_tile = CONV_TT // CONV_HALO
    halo_per_seq = seq // CONV_HALO
    return pl.pallas_call(
        _conv_p_body,
        grid=(batch, nt),
        in_specs=[pl.BlockSpec((CONV_HALO, CONV_DIM),
                               lambda bb, i: (bb * halo_per_seq + jnp.maximum(i * halo_per_tile - 1, 0), 0)),
                  pl.BlockSpec((CONV_TT, CONV_DIM), lambda bb, i: (bb * nt + i, 0)),
                  _const_spec((CONV_WIDTH, 8, CONV_DIM)), _const_spec((1, CONV_DIM)),
                  _const_spec((1, CONV_DIM)), _const_spec((1, CONV_DIM))],
        out_specs=pl.BlockSpec((CONV_TT, CONV_DIM), lambda bb, i: (bb * nt + i, 0)),
        out_shape=jax.ShapeDtypeStruct((batch * seq, CONV_DIM), BF16),
        scratch_shapes=[pltpu.VMEM((CONV_HALO + CONV_TT, CONV_DIM), F32)],
        compiler_params=_params(("parallel", "arbitrary")),
        name="conv_prompt",
    )(u, u, w8, b, g, beta)


def _conv_s_body(ext_ref, w_ref, b_ref, g_ref, beta_ref, o_ref):
    td, bd, _ = o_ref.shape
    for rg in range(bd // CONV_R):
        for t in range(td):
            acc = jnp.zeros((CONV_R // 8, 8, CONV_DIM), F32)
            for k in range(CONV_WIDTH):
                xs = ext_ref[t + k, rg * CONV_R:(rg + 1) * CONV_R, :].reshape(CONV_R // 8, 8, CONV_DIM)
                acc = acc + xs * w_ref[k][None]
            y = acc.reshape(CONV_R, CONV_DIM) + b_ref[...]
            o_ref[t, rg * CONV_R:(rg + 1) * CONV_R, :] = _ln_silu(
                y, g_ref[...], beta_ref[...]).astype(BF16)


def _conv_sample(ext_t, w8, b, g, beta):
    n_ext, bd, _ = ext_t.shape
    td = n_ext - (CONV_WIDTH - 1)
    return pl.pallas_call(
        _conv_s_body,
        out_shape=jax.ShapeDtypeStruct((td, bd, CONV_DIM), BF16),
        compiler_params=pltpu.CompilerParams(vmem_limit_bytes=VMEM_LIMIT),
        name="conv_sample",
    )(ext_t, w8, b, g, beta)


def _mix_body(x_ref, ol_ref, cv_ref, wv_ref, wout_ref, cg_ref, wcq_ref, x1_ref, qc_ref):
    half = N_HEADS * V_DIM
    mla = jnp.dot(ol_ref[...], wv_ref[...], preferred_element_type=F32).astype(BF16)
    x1 = (x_ref[...]
          + jnp.dot(mla, wout_ref[0:half, :], preferred_element_type=F32)
          + jnp.dot(cv_ref[...], wout_ref[half:, :], preferred_element_type=F32))
    x1_ref[...] = x1
    hq = _rms(x1, cg_ref[...]).astype(BF16)
    qc_ref[...] = (jnp.dot(hq, wcq_ref[...], preferred_element_type=F32) * MEM_SCALE).astype(BF16)


def _mix(x, olat, convo, wvbd, wout, cg, wcq):
    n = x.shape[0]
    row = lambda w: pl.BlockSpec((TM, w), lambda i: (i, 0))
    return pl.pallas_call(
        _mix_body,
        grid=(n // TM,),
        in_specs=[row(D_MODEL), row(N_HEADS * KV_LORA), row(CONV_DIM),
                  _const_spec((N_HEADS * KV_LORA, N_HEADS * V_DIM)),
                  _const_spec((D_MODEL, D_MODEL)), _const_spec((1, D_MODEL)),
                  _const_spec((D_MODEL, D_MODEL))],
        out_specs=[row(D_MODEL), row(D_MODEL)],
        out_shape=[jax.ShapeDtypeStruct((n, D_MODEL), F32),
                   jax.ShapeDtypeStruct((n, D_MODEL), BF16)],
        compiler_params=_params(("parallel",)),
        name="mix_out",
    )(x, olat, convo, wvbd, wout, cg, wcq)


def _memkv_body(mem_ref, g_ref, wk_ref, wv_ref, k_ref, v_ref):
    mn = _rms(mem_ref[...], g_ref[...]).astype(BF16)
    k_ref[...] = jnp.dot(mn, wk_ref[...], preferred_element_type=F32)
    v_ref[...] = jnp.dot(mn, wv_ref[...], preferred_element_type=F32)


def _memkv(mem, g, wk, wv):
    n = mem.shape[0]
    row = pl.BlockSpec((N_MEM, D_MODEL), lambda i: (i, 0))
    return pl.pallas_call(
        _memkv_body,
        grid=(n // N_MEM,),
        in_specs=[row, _const_spec((1, D_MODEL)), _const_spec((D_MODEL, D_MODEL)),
                  _const_spec((D_MODEL, D_MODEL))],
        out_specs=[row, row],
        out_shape=[jax.ShapeDtypeStruct((n, D_MODEL), F32)] * 2,
        compiler_params=_params(("parallel",)),
        name="memory_kv",
    )(mem, g, wk, wv)


def _cross_head(q, k, v):
    s = lax.dot_general(q, k.astype(BF16), (((1,), (1,)), ((), ())), preferred_element_type=F32)
    e = jnp.exp(s - jnp.max(s, axis=1, keepdims=True))
    p = (e * (1.0 / jnp.sum(e, axis=1, keepdims=True))).astype(BF16)
    return jnp.dot(p, v.astype(BF16), preferred_element_type=F32).astype(BF16)


def _cross_p_body(q_ref, k_ref, v_ref, o_ref):
    for hh in range(MEM_HEADS):
        sl = slice(hh * MEM_HEAD_DIM, (hh + 1) * MEM_HEAD_DIM)
        o_ref[:, sl] = _cross_head(q_ref[:, sl], k_ref[:, sl], v_ref[:, sl])


def _cross_s_body(q_ref, k_ref, v_ref, o_ref):
    for hh in range(MEM_HEADS):
        sl = slice(hh * MEM_HEAD_DIM, (hh + 1) * MEM_HEAD_DIM)
        o_ref[:, sl] = _cross_head(q_ref[:, sl], k_ref[:, hh, :], v_ref[:, hh, :])


def _cross_prompt(qc, mk, mv, batch, seq):
    nt = seq // TM
    kv = pl.BlockSpec((N_MEM, D_MODEL), lambda b, i: (b, 0))
    row = pl.BlockSpec((TM, D_MODEL), lambda b, i: (b * nt + i, 0))
    return pl.pallas_call(
        _cross_p_body,
        grid=(batch, nt),
        in_specs=[row, kv, kv],
        out_specs=row,
        out_shape=jax.ShapeDtypeStruct((batch * seq, D_MODEL), BF16),
        compiler_params=_params(("parallel", "arbitrary")),
        name="cross_prompt",
    )(qc, mk, mv)


def _cross_sample(qc_pad, mk, mv, layer):
    bd, rows, _ = qc_pad.shape
    kv = pl.BlockSpec((None, None, N_MEM, MEM_HEADS, MEM_HEAD_DIM), lambda b: (layer, b, 0, 0, 0))
    row = pl.BlockSpec((None, rows, D_MODEL), lambda b: (b, 0, 0))
    return pl.pallas_call(
        _cross_s_body,
        grid=(bd,),
        in_specs=[row, kv, kv],
        out_specs=row,
        out_shape=jax.ShapeDtypeStruct((bd, rows, D_MODEL), BF16),
        compiler_params=_params(("parallel",)),
        name="cross_sample",
    )(qc_pad, mk, mv)


def _post_body(x1_ref, o_ref, wco_ref, g_ref, wg_ref, wu_ref, wd_ref, fn_ref, y_ref):
    x2 = x1_ref[...] + jnp.dot(o_ref[...], wco_ref[...], preferred_element_type=F32)
    h = _rms(x2, g_ref[...]).astype(BF16)
    x3 = x2 + 0.5 * _swiglu(h, wg_ref, wu_ref, wd_ref)
    y_ref[...] = _rms(x3, fn_ref[...])


def _post(x1, o, wco, g, wg, wu, wd, fn):
    n = x1.shape[0]
    row = pl.BlockSpec((TM, D_MODEL), lambda i: (i, 0))
    return pl.pallas_call(
        _post_body,
        grid=(n // TM,),
        in_specs=[row, row, _const_spec((D_MODEL, D_MODEL)), _const_spec((1, D_MODEL)),
                  _const_spec((D_MODEL, D_FF)), _const_spec((D_MODEL, D_FF)),
                  _const_spec((D_FF, D_MODEL)), _const_spec((1, D_MODEL))],
        out_specs=row,
        out_shape=jax.ShapeDtypeStruct((n, D_MODEL), F32),
        compiler_params=_params(("parallel",)),
        name="cross_out_ffn2",
    )(x1, o, wco, g, wg, wu, wd, fn)


def _rope_tables(pos):
    half = QK_ROPE // 2
    inv = ROPE_BASE ** (-jnp.arange(half, dtype=F32) / half)
    ang = pos.astype(F32)[:, None] * inv[None, :]
    cos = jnp.cos(ang)
    sin = jnp.sin(ang)
    cos_t = jnp.tile(jnp.concatenate([cos, cos], axis=1), (1, N_HEADS))
    sin_t = jnp.tile(jnp.concatenate([-sin, sin], axis=1), (1, N_HEADS))
    return cos_t, sin_t


def _swap_halves(w):
    half = QK_ROPE // 2
    return jnp.concatenate([w[..., half:], w[..., :half]], axis=-1)


def _layer_weights(l, ffn1_norm, ffn1_w_gate, ffn1_w_up, ffn1_w_down, mix_norm, w_in, q_norm,
                   w_q_up, kv_norm, w_k_up, w_v_up, conv_w, conv_b, conv_ln_g, conv_ln_b, w_out,
                   cross_norm, mem_norm, w_cq, w_mk, w_mv, w_co, ffn2_norm, ffn2_w_gate,
                   ffn2_w_up, ffn2_w_down):
    row = lambda v: v[l][None, :]
    wi = w_in[l]
    o1, o2, o3 = Q_LORA, Q_LORA + KV_LORA, Q_LORA + KV_LORA + QK_ROPE
    w_kr = wi[:, o2:o3]
    rep = KR_TILE // QK_ROPE
    win = jnp.concatenate(
        [wi[:, :o1], wi[:, o1:o2], wi[:, o3:o3 + CONV_DIM], wi[:, o3 + CONV_DIM:],
         jnp.tile(w_kr, (1, rep)), jnp.tile(_swap_halves(w_kr), (1, rep))], axis=1).astype(BF16)
    wq3 = w_q_up[l].reshape(Q_LORA, N_HEADS, QK_NOPE + QK_ROPE)
    wq_nope = jnp.pad(wq3[:, :, :QK_NOPE], ((0, 0), (0, 0), (0, NOPE_PAD - QK_NOPE)))
    wq_rope = wq3[:, :, QK_NOPE:]
    wq = jnp.concatenate(
        [wq_nope.reshape(Q_LORA, N_HEADS * NOPE_PAD), wq_rope.reshape(Q_LORA, ROPE_ALL),
         _swap_halves(wq_rope).reshape(Q_LORA, ROPE_ALL)], axis=1).astype(BF16)
    wk = jnp.pad(jnp.transpose(w_k_up[l], (1, 2, 0)),
                 ((0, 0), (0, NOPE_PAD - QK_NOPE), (0, 0))).astype(BF16)
    wv = w_v_up[l]
    eye = jnp.eye(N_HEADS, dtype=wv.dtype)
    wvbd = jnp.einsum('chv,hg->hcgv', wv, eye).reshape(
        N_HEADS * KV_LORA, N_HEADS * V_DIM).astype(BF16)
    w8 = jnp.broadcast_to(conv_w[l][:, None, :], (CONV_WIDTH, 8, CONV_DIM))
    return dict(
        ffn1=(row(ffn1_norm), ffn1_w_gate[l].astype(BF16), ffn1_w_up[l].astype(BF16),
              ffn1_w_down[l].astype(BF16)),
        proj=(row(mix_norm), win, row(q_norm), wq, row(kv_norm), wk),
        conv=(w8, row(conv_b), row(conv_ln_g), row(conv_ln_b)),
        mix=(wvbd, w_out[l].astype(BF16), row(cross_norm), w_cq[l].astype(BF16)),
        mem=(row(mem_norm), w_mk[l].astype(BF16), w_mv[l].astype(BF16)),
        post=(w_co[l].astype(BF16), row(ffn2_norm), ffn2_w_gate[l].astype(BF16),
              ffn2_w_up[l].astype(BF16), ffn2_w_down[l].astype(BF16)),
    )


def kernel(x_prompt, x_sample, mem_prompt, cache_kv_latent, cache_k_rope, state_conv, cache_mem_k, cache_mem_v, page_table, ffn1_norm, ffn1_w_gate, ffn1_w_up, ffn1_w_down, mix_norm, w_in, q_norm, w_q_up, kv_norm, w_k_up, w_v_up, conv_w, conv_b, conv_ln_g, conv_ln_b, w_out, cross_norm, mem_norm, w_cq, w_mk, w_mv, w_co, ffn2_norm, ffn2_w_gate, ffn2_w_up, ffn2_w_down, final_norm):
    batch, seq, _ = x_prompt.shape
    bd, td, _ = x_sample.shape
    depth = ffn1_norm.shape[0]
    assert depth == 1, "the final norm is fused into the last layer's kernel"
    past_len = page_table.shape[1] * PAGE_SIZE
    n_p, n_s = batch * seq, bd * td
    state_w = CONV_WIDTH - 1

    cos_p, sin_p = _rope_tables(jnp.arange(seq, dtype=jnp.int32))
    cos_s, sin_s = _rope_tables(past_len + jnp.arange(td, dtype=jnp.int32))
    cos_s, sin_s = jnp.tile(cos_s, (TM // td, 1)), jnp.tile(sin_s, (TM // td, 1))

    xp = x_prompt.reshape(n_p, D_MODEL)
    xs = x_sample.reshape(n_s, D_MODEL)
    fn = final_norm[None, :]
    outs = {k: [] for k in ("kvl_p", "kr_p", "cs_p", "mk_p", "mv_p", "kvl_s", "kr_s", "cs_s")}
    for l in range(depth):
        w = _layer_weights(l, ffn1_norm, ffn1_w_gate, ffn1_w_up, ffn1_w_down, mix_norm, w_in,
                           q_norm, w_q_up, kv_norm, w_k_up, w_v_up, conv_w, conv_b, conv_ln_g,
                           conv_ln_b, w_out, cross_norm, mem_norm, w_cq, w_mk, w_mv, w_co,
                           ffn2_norm, ffn2_w_gate, ffn2_w_up, ffn2_w_down)
        xp = _ffn(xp, *w["ffn1"])
        xs = _ffn(xs, *w["ffn1"])

        qlat, qrope, kfull, ckv, krope, u = _proj(xp, cos_p, sin_p, seq // TM, *w["proj"])
        olat = _attn_prompt(qlat, qrope, kfull, batch, seq)
        convo = _conv_prompt(u, *w["conv"], batch, seq)
        xp1, qc_p = _mix(xp, olat, convo, *w["mix"])
        outs["kvl_p"].append(ckv.reshape(batch, seq, KV_LORA))
        outs["kr_p"].append(krope.reshape(batch, seq, QK_ROPE))
        outs["cs_p"].append(u.reshape(batch, seq, CONV_DIM)[:, seq - state_w:])

        qlat, qrope, _, ckv, krope, u = _proj(xs, cos_s, sin_s, 1, *w["proj"])
        rows = td * N_HEADS
        olat = _attn_decode(
            page_table,
            qlat.reshape(bd, rows, KV_LORA), qrope.reshape(bd, rows, QK_ROPE),
            ckv.reshape(bd, td, KV_LORA), krope.reshape(bd, td, QK_ROPE),
            cache_kv_latent, cache_k_rope, l).reshape(n_s, N_HEADS * KV_LORA)
        u3 = u.reshape(bd, td, CONV_DIM)
        ext = jnp.concatenate([state_conv[l], u3], axis=1)
        convo = _conv_sample(jnp.transpose(ext, (1, 0, 2)), *w["conv"])
        convo = jnp.transpose(convo, (1, 0, 2)).reshape(n_s, CONV_DIM)
        xs1, qc_s = _mix(xs, olat, convo, *w["mix"])
        outs["kvl_s"].append(ckv.reshape(bd, td, KV_LORA))
        outs["kr_s"].append(krope.reshape(bd, td, QK_ROPE))
        outs["cs_s"].append(ext[:, td:])

        mk, mv = _memkv(mem_prompt.reshape(batch * N_MEM, D_MODEL), *w["mem"])
        o_p = _cross_prompt(qc_p, mk, mv, batch, seq)
        q_rows = 8
        qc_pad = jnp.pad(qc_s.reshape(bd, td, D_MODEL), ((0, 0), (0, q_rows - td), (0, 0)))
        o_s = _cross_sample(qc_pad, cache_mem_k, cache_mem_v, l)
        o_s = o_s[:, :td].reshape(n_s, D_MODEL)
        outs["mk_p"].append(mk.reshape(batch, N_MEM, MEM_HEADS, MEM_HEAD_DIM))
        outs["mv_p"].append(mv.reshape(batch, N_MEM, MEM_HEADS, MEM_HEAD_DIM))

        xp = _post(xp1, o_p, *w["post"], fn)
        xs = _post(xs1, o_s, *w["post"], fn)

    st = lambda k: jnp.stack(outs[k])
    return (xp.reshape(batch, seq, D_MODEL), xs.reshape(bd, td, D_MODEL),
            st("kvl_p"), st("kr_p"), st("cs_p"), st("mk_p"), st("mv_p"),
            st("kvl_s"), st("kr_s"), st("cs_s"))
```

```python
import functools

import jax
import jax.numpy as jnp
from jax import lax
from jax.experimental import pallas as pl
from jax.experimental.pallas import tpu as pltpu

F32 = jnp.float32
BF16 = jnp.bfloat16

D_MODEL = 1024
N_HEADS = 8
QK_NOPE = 64
QK_ROPE = 32
V_DIM = 64
Q_LORA = 384
KV_LORA = 256
CONV_DIM = 512
CONV_WIDTH = 31
D_FF = 2816
N_MEM = 256
MEM_HEADS = 4
MEM_HEAD_DIM = D_MODEL // MEM_HEADS
PAGE_SIZE = 128
ROPE_BASE = 10000.0
EPS = 1e-6
NEG_INF = -1e30
ATTN_SCALE = (QK_NOPE + QK_ROPE) ** -0.5
MEM_SCALE = MEM_HEAD_DIM ** -0.5
QK_LOG2_SCALE = ATTN_SCALE * 1.4426950408889634

VMEM_LIMIT = 56 * 1024 * 1024

TM = 512
NOPE_PAD = 128
ROPE_ALL = N_HEADS * QK_ROPE
KR_TILE = 128
K_FULL = KV_LORA + KR_TILE
O_CQ, O_CKV, O_A, O_GT, O_KR, O_KRS, W_IN_COLS = 0, 384, 640, 1152, 1664, 1792, 1920
O_QN, O_QR, O_QRS, W_Q_COLS = 0, 1024, 1280, 1536

BQ = 128
BK = 512
PAGES_PER_STEP = 64
DECODE_SUBCHUNKS = 4
CROSS_TD = 4
CONV_TT = 256
CONV_R = 32
CONV_HALO = 32


def _const_spec(shape):
    nd = len(shape)
    return pl.BlockSpec(shape, lambda *_: (0,) * nd, pipeline_mode=pl.Buffered(1))


def _params(sem):
    return pltpu.CompilerParams(dimension_semantics=sem, vmem_limit_bytes=VMEM_LIMIT)


def _rms(x, g):
    ms = jnp.mean(x * x, axis=-1, keepdims=True)
    return x * lax.rsqrt(ms + EPS) * g


def _swiglu(h, wg_ref, wu_ref, wd_ref):
    gate = jnp.dot(h, wg_ref[...], preferred_element_type=F32)
    up = jnp.dot(h, wu_ref[...], preferred_element_type=F32)
    a = (gate * jax.nn.sigmoid(gate) * up).astype(BF16)
    return jnp.dot(a, wd_ref[...], preferred_element_type=F32)


def _ffn_body(x_ref, g_ref, wg_ref, wu_ref, wd_ref, o_ref):
    x = x_ref[...]
    h = _rms(x, g_ref[...]).astype(BF16)
    o_ref[...] = x + 0.5 * _swiglu(h, wg_ref, wu_ref, wd_ref)


def _ffn(x, g, wg, wu, wd):
    n = x.shape[0]
    row = pl.BlockSpec((TM, D_MODEL), lambda i: (i, 0))
    return pl.pallas_call(
        _ffn_body,
        grid=(n // TM,),
        in_specs=[row, _const_spec((1, D_MODEL)), _const_spec((D_MODEL, D_FF)),
                  _const_spec((D_MODEL, D_FF)), _const_spec((D_FF, D_MODEL))],
        out_specs=row,
        out_shape=jax.ShapeDtypeStruct((n, D_MODEL), F32),
        compiler_params=_params(("parallel",)),
        name="ffn1",
    )(x, g, wg, wu, wd)


def _proj_body(x_ref, cos_ref, sin_ref, g_ref, win_ref, qn_ref, wq_ref, kvn_ref, wk_ref,
               qlat_ref, qrope_ref, kfull_ref, ckv_ref, krope_ref, u_ref):
    h = _rms(x_ref[...], g_ref[...]).astype(BF16)
    z = jnp.dot(h, win_ref[...], preferred_element_type=F32)
    cos = cos_ref[...]
    sin = sin_ref[...]
    u_ref[...] = z[:, O_A:O_GT] * jax.nn.sigmoid(z[:, O_GT:O_KR])
    c = _rms(z[:, O_CKV:O_A], kvn_ref[...])
    ckv_ref[...] = c
    kr = z[:, O_KR:O_KRS] * cos[:, :KR_TILE] + z[:, O_KRS:W_IN_COLS] * sin[:, :KR_TILE]
    krope_ref[...] = kr[:, :QK_ROPE]
    kfull_ref[:, 0:KV_LORA] = c.astype(BF16)
    kfull_ref[:, KV_LORA:K_FULL] = kr.astype(BF16)
    cqn = _rms(z[:, O_CQ:O_CKV], qn_ref[...]).astype(BF16)
    q = jnp.dot(cqn, wq_ref[...], preferred_element_type=F32)
    qrope_ref[...] = ((q[:, O_QR:O_QRS] * cos + q[:, O_QRS:W_Q_COLS] * sin)
                      * QK_LOG2_SCALE).astype(BF16)
    for hh in range(N_HEADS):
        qn = q[:, hh * NOPE_PAD:(hh + 1) * NOPE_PAD].astype(BF16)
        qlat_ref[:, hh * KV_LORA:(hh + 1) * KV_LORA] = (jnp.dot(
            qn, wk_ref[hh], preferred_element_type=F32) * QK_LOG2_SCALE).astype(BF16)


def _proj(x, cos, sin, tab_blocks, g, win, qn, wq, kvn, wk):
    n = x.shape[0]
    row = lambda w: pl.BlockSpec((TM, w), lambda i: (i, 0))
    tab = pl.BlockSpec((TM, ROPE_ALL), lambda i: (i % tab_blocks, 0))
    return pl.pallas_call(
        _proj_body,
        grid=(n // TM,),
        in_specs=[row(D_MODEL), tab, tab, _const_spec((1, D_MODEL)),
                  _const_spec((D_MODEL, W_IN_COLS)), _const_spec((1, Q_LORA)),
                  _const_spec((Q_LORA, W_Q_COLS)), _const_spec((1, KV_LORA)),
                  _const_spec((N_HEADS, NOPE_PAD, KV_LORA))],
        out_specs=[row(N_HEADS * KV_LORA), row(ROPE_ALL), row(K_FULL), row(KV_LORA),
                   row(QK_ROPE), row(CONV_DIM)],
        out_shape=[jax.ShapeDtypeStruct((n, N_HEADS * KV_LORA), BF16),
                   jax.ShapeDtypeStruct((n, ROPE_ALL), BF16),
                   jax.ShapeDtypeStruct((n, K_FULL), BF16),
                   jax.ShapeDtypeStruct((n, KV_LORA), F32),
                   jax.ShapeDtypeStruct((n, QK_ROPE), F32),
                   jax.ShapeDtypeStruct((n, CONV_DIM), F32)],
        compiler_params=_params(("parallel",)),
        name="mixer_proj",
    )(x, cos, sin, g, win, qn, wq, kvn, wk)


def _lane_tile(v, n):
    return jnp.concatenate([v] * n, axis=1) if n > 1 else v


def _attn_p_body(qlat_ref, qrope_ref, k_ref, o_ref, qs_ref, m_ref, l_ref, acc_ref):
    i = pl.program_id(1)
    rows = N_HEADS * BQ
    lane = lax.broadcasted_iota(jnp.int32, (BQ, KR_TILE), 1)
    heads_per_tile = KR_TILE // QK_ROPE
    for hh in range(N_HEADS):
        r0 = hh * BQ
        qs_ref[r0:r0 + BQ, 0:KV_LORA] = qlat_ref[:, hh * KV_LORA:(hh + 1) * KV_LORA]
        t = hh // heads_per_tile
        grp = qrope_ref[:, t * KR_TILE:(t + 1) * KR_TILE]
        lo = (hh % heads_per_tile) * QK_ROPE
        keep = jnp.logical_and(lane >= lo, lane < lo + QK_ROPE)
        qs_ref[r0:r0 + BQ, KV_LORA:K_FULL] = jnp.where(keep, grp, jnp.zeros_like(grp))
    m_ref[...] = jnp.full(m_ref.shape, NEG_INF, F32)
    l_ref[...] = jnp.zeros(l_ref.shape, F32)
    acc_ref[...] = jnp.zeros(acc_ref.shape, F32)
    q0 = i * BQ
    n_full = q0 // BK

    def step(kb, masked):
        start = pl.multiple_of(kb * BK, BK)
        k = k_ref[pl.ds(start, BK), :]
        s = lax.dot_general(qs_ref[...], k, (((1,), (1,)), ((), ())),
                            preferred_element_type=F32)
        if masked:
            r = lax.broadcasted_iota(jnp.int32, (rows, BK), 0)
            col = lax.broadcasted_iota(jnp.int32, (rows, BK), 1)
            qpos = q0 + jnp.bitwise_and(r, BQ - 1)
            s = jnp.where(start + col <= qpos, s, NEG_INF)
        m_prev = m_ref[...]
        m_next = jnp.maximum(m_prev, jnp.max(s, axis=1, keepdims=True))
        alpha = jnp.exp2(m_prev - m_next)
        p = jnp.exp2(s - _lane_tile(m_next, BK // 128))
        l_ref[...] = alpha * l_ref[...] + jnp.sum(p, axis=1, keepdims=True)
        m_ref[...] = m_next
        pv = jnp.dot(p.astype(BF16), k[:, 0:KV_LORA], preferred_element_type=F32)
        acc_ref[...] = acc_ref[...] * _lane_tile(alpha, KV_LORA // 128) + pv

    def full_step(kb, carry):
        step(kb, False)
        return carry

    lax.fori_loop(0, n_full, full_step, 0)
    step(n_full, True)
    inv = 1.0 / l_ref[...]
    for hh in range(N_HEADS):
        r0 = hh * BQ
        o_ref[:, hh * KV_LORA:(hh + 1) * KV_LORA] = (
            acc_ref[r0:r0 + BQ, :] * _lane_tile(inv[r0:r0 + BQ, :], KV_LORA // 128)).astype(BF16)


def _attn_prompt(qlat, qrope, kfull, batch, seq):
    nq = seq // BQ
    rows = N_HEADS * BQ
    return pl.pallas_call(
        _attn_p_body,
        grid=(batch, nq),
        in_specs=[pl.BlockSpec((BQ, N_HEADS * KV_LORA), lambda b, i: (b * nq + i, 0)),
                  pl.BlockSpec((BQ, ROPE_ALL), lambda b, i: (b * nq + i, 0)),
                  pl.BlockSpec((seq, K_FULL), lambda b, i: (b, 0))],
        out_specs=pl.BlockSpec((BQ, N_HEADS * KV_LORA), lambda b, i: (b * nq + i, 0)),
        out_shape=jax.ShapeDtypeStruct((batch * seq, N_HEADS * KV_LORA), BF16),
        scratch_shapes=[pltpu.VMEM((rows, K_FULL), BF16), pltpu.VMEM((rows, 128), F32),
                        pltpu.VMEM((rows, 128), F32), pltpu.VMEM((rows, KV_LORA), F32)],
        compiler_params=_params(("parallel", "arbitrary")),
        name="attn_prompt",
    )(qlat, qrope, kfull)


def _attn_d_body(pt_ref, qlat_ref, qr_ref, cnew_ref, krnew_ref, cc_hbm, ckr_hbm, o_ref,
                 cbuf, krb, kbuf, krbuf, m_ref, l_ref, acc_ref, sem_c, sem_k, *, layer):
    g_pages = cbuf.shape[1]
    b = pl.program_id(0)
    j = pl.program_id(1)
    nj = pl.num_programs(1)
    n_steps = pl.num_programs(0) * nj
    t = b * nj + j
    slot = lax.rem(t, 2)
    rows = qlat_ref.shape[0]
    td = rows // N_HEADS

    def page_copies(step, sl, g):
        page = pt_ref[step * g_pages + g]
        return (pltpu.make_async_copy(cc_hbm.at[layer, page], cbuf.at[sl, g], sem_c.at[sl]),
                pltpu.make_async_copy(ckr_hbm.at[layer, page], krb.at[sl, g], sem_k.at[sl]))

    def start_step(step, sl):
        def issue(g, carry):
            for cp in page_copies(step, sl, g):
                cp.start()
            return carry
        lax.fori_loop(0, g_pages, issue, 0)

    @pl.when(t == 0)
    def _():
        start_step(0, 0)

    @pl.when(t + 1 < n_steps)
    def _():
        start_step(t + 1, 1 - slot)

    def wait_page(g, carry):
        for cp in page_copies(t, slot, g):
            cp.wait()
        return carry
    lax.fori_loop(0, g_pages, wait_page, 0)

    @pl.when(j == 0)
    def _():
        m_ref[...] = jnp.full(m_ref.shape, NEG_INF, F32)
        l_ref[...] = jnp.zeros(l_ref.shape, F32)
        acc_ref[...] = jnp.zeros(acc_ref.shape, F32)

    for g in range(g_pages):
        kbuf[g * PAGE_SIZE:(g + 1) * PAGE_SIZE, :] = cbuf[slot, g].astype(BF16)
        krbuf[:, g * PAGE_SIZE:(g + 1) * PAGE_SIZE] = krb[slot, g].astype(BF16)
    q = qlat_ref[...]
    qr = qr_ref[...]
    sub = g_pages * PAGE_SIZE // DECODE_SUBCHUNKS
    m_next = m_ref[...]
    l_new = l_ref[...]
    acc = acc_ref[...]
    for h in range(DECODE_SUBCHUNKS):
        ks = kbuf[h * sub:(h + 1) * sub, :]
        s = (lax.dot_general(q, ks, (((1,), (1,)), ((), ())), preferred_element_type=F32)
             + jnp.dot(qr, krbuf[:, h * sub:(h + 1) * sub], preferred_element_type=F32))
        m_prev = m_next
        m_next = jnp.maximum(m_prev, jnp.max(s, axis=1, keepdims=True))
        alpha = jnp.exp2(m_prev - m_next)
        p = jnp.exp2(s - _lane_tile(m_next, sub // 128))
        l_new = alpha * l_new + jnp.sum(p, axis=1, keepdims=True)
        acc = (acc * _lane_tile(alpha, KV_LORA // 128)
               + jnp.dot(p.astype(BF16), ks, preferred_element_type=F32))
    l_ref[...] = l_new
    m_ref[...] = m_next
    acc_ref[...] = acc

    @pl.when(j == nj - 1)
    def _():
        qf = q.astype(F32)
        qrf = qr.astype(F32)
        tok = lax.broadcasted_iota(jnp.int32, (rows, 1), 0) // N_HEADS
        s_new = []
        for t2 in range(td):
            cn = cnew_ref[t2:t2 + 1, :].astype(BF16).astype(F32)
            kn = krnew_ref[t2:t2 + 1, :].astype(BF16).astype(F32)
            st = jnp.sum(qf * cn, axis=1, keepdims=True) + jnp.sum(qrf * kn, axis=1, keepdims=True)
            s_new.append(jnp.where(t2 <= tok, st, NEG_INF))
        m_prev = m_ref[...]
        m_cur = s_new[0]
        for st in s_new[1:]:
            m_cur = jnp.maximum(m_cur, st)
        m_fin = jnp.maximum(m_prev, m_cur)
        alpha = jnp.exp2(m_prev - m_fin)
        l_fin = alpha * l_ref[...]
        acc_fin = acc_ref[...] * _lane_tile(alpha, KV_LORA // 128)
        for t2 in range(td):
            pt = jnp.exp2(s_new[t2] - m_fin)
            l_fin = l_fin + pt
            cn = cnew_ref[t2:t2 + 1, :].astype(BF16).astype(F32)
            acc_fin = acc_fin + _lane_tile(pt.astype(BF16).astype(F32), KV_LORA // 128) * cn
        o_ref[...] = (acc_fin * _lane_tile(1.0 / l_fin, KV_LORA // 128)).astype(BF16)


def _attn_decode(page_table, qlat, qrope, cnew, krnew, cache_c, cache_kr_t, layer):
    bd, rows, _ = qlat.shape
    td = rows // N_HEADS
    n_pages = page_table.shape[1]
    g_pages = PAGES_PER_STEP
    nj = n_pages // g_pages
    per_b = lambda r, w: pl.BlockSpec((None, r, w), lambda b, j, pt: (b, 0, 0))
    hbm = pl.BlockSpec(memory_space=pl.ANY)
    grid_spec = pltpu.PrefetchScalarGridSpec(
        num_scalar_prefetch=1,
        grid=(bd, nj),
        in_specs=[per_b(rows, KV_LORA), per_b(rows, QK_ROPE), per_b(td, KV_LORA),
                  per_b(td, QK_ROPE), hbm, hbm],
        out_specs=per_b(rows, KV_LORA),
        scratch_shapes=[pltpu.VMEM((2, g_pages, PAGE_SIZE, KV_LORA), F32),
                        pltpu.VMEM((2, g_pages, QK_ROPE, PAGE_SIZE), F32),
                        pltpu.VMEM((g_pages * PAGE_SIZE, KV_LORA), BF16),
                        pltpu.VMEM((QK_ROPE, g_pages * PAGE_SIZE), BF16),
                        pltpu.VMEM((rows, 128), F32), pltpu.VMEM((rows, 128), F32),
                        pltpu.VMEM((rows, KV_LORA), F32),
                        pltpu.SemaphoreType.DMA((2,)), pltpu.SemaphoreType.DMA((2,))])
    return pl.pallas_call(
        functools.partial(_attn_d_body, layer=layer),
        grid_spec=grid_spec,
        out_shape=jax.ShapeDtypeStruct((bd, rows, KV_LORA), BF16),
        compiler_params=_params(("arbitrary", "arbitrary")),
        name="attn_decode",
    )(page_table.reshape(-1), qlat, qrope, cnew, krnew, cache_c, cache_kr_t)


def _ln_silu(y, g, b):
    mu = jnp.mean(y, axis=-1, keepdims=True)
    d = y - mu
    var = jnp.mean(d * d, axis=-1, keepdims=True)
    z = d * lax.rsqrt(var + EPS) * g + b
    return z * jax.nn.sigmoid(z)


def _conv_p_body(prev_ref, cur_ref, w_ref, b_ref, g_ref, beta_ref, o_ref, xb_ref):
    i = pl.program_id(1)
    xb_ref[0:CONV_HALO, :] = jnp.where(i > 0, prev_ref[...], 0.0)
    xb_ref[CONV_HALO:CONV_HALO + CONV_TT, :] = cur_ref[...]
    off = CONV_HALO - (CONV_WIDTH - 1)
    for r in range(CONV_TT // CONV_R):
        acc = jnp.zeros((CONV_R // 8, 8, CONV_DIM), F32)
        for k in range(CONV_WIDTH):
            lo = r * CONV_R + off + k
            xs = xb_ref[lo:lo + CONV_R, :].reshape(CONV_R // 8, 8, CONV_DIM)
            acc = acc + xs * w_ref[k][None]
        y = acc.reshape(CONV_R, CONV_DIM) + b_ref[...]
        o_ref[r * CONV_R:(r + 1) * CONV_R, :] = _ln_silu(y, g_ref[...], beta_ref[...]).astype(BF16)


def _conv_prompt(u, w8, b, g, beta, batch, seq):
    nt = seq // CONV_TT
    halo_per_tile = CONV_TT // CONV_HALO
    halo_per_seq = seq // CONV_HALO
    return pl.pallas_call(
        _conv_p_body,
        grid=(batch, nt),
        in_specs=[pl.BlockSpec((CONV_HALO, CONV_DIM),
                               lambda bb, i: (bb * halo_per_seq + jnp.maximum(i * halo_per_tile - 1, 0), 0)),
                  pl.BlockSpec((CONV_TT, CONV_DIM), lambda bb, i: (bb * nt + i, 0)),
                  _const_spec((CONV_WIDTH, 8, CONV_DIM)), _const_spec((1, CONV_DIM)),
                  _const_spec((1, CONV_DIM)), _const_spec((1, CONV_DIM))],
        out_specs=pl.BlockSpec((CONV_TT, CONV_DIM), lambda bb, i: (bb * nt + i, 0)),
        out_shape=jax.ShapeDtypeStruct((batch * seq, CONV_DIM), BF16),
        scratch_shapes=[pltpu.VMEM((CONV_HALO + CONV_TT, CONV_DIM), F32)],
        compiler_params=_params(("parallel", "arbitrary")),
        name="conv_prompt",
    )(u, u, w8, b, g, beta)


def _conv_s_body(ext_ref, w_ref, b_ref, g_ref, beta_ref, o_ref):
    td, bd, _ = o_ref.shape
    for rg in range(bd // CONV_R):
        for t in range(td):
            acc = jnp.zeros((CONV_R // 8, 8, CONV_DIM), F32)
            for k in range(CONV_WIDTH):
                xs = ext_ref[t + k, rg * CONV_R:(rg + 1) * CONV_R, :].reshape(CONV_R // 8, 8, CONV_DIM)
                acc = acc + xs * w_ref[k][None]
            y = acc.reshape(CONV_R, CONV_DIM) + b_ref[...]
            o_ref[t, rg * CONV_R:(rg + 1) * CONV_R, :] = _ln_silu(
                y, g_ref[...], beta_ref[...]).astype(BF16)


def _conv_sample(ext_t, w8, b, g, beta):
    n_ext, bd, _ = ext_t.shape
    td = n_ext - (CONV_WIDTH - 1)
    return pl.pallas_call(
        _conv_s_body,
        out_shape=jax.ShapeDtypeStruct((td, bd, CONV_DIM), BF16),
        compiler_params=pltpu.CompilerParams(vmem_limit_bytes=VMEM_LIMIT),
        name="conv_sample",
    )(ext_t, w8, b, g, beta)


def _mix_body(x_ref, ol_ref, cv_ref, wv_ref, wout_ref, cg_ref, wcq_ref, x1_ref, qc_ref):
    half = N_HEADS * V_DIM
    mla = jnp.dot(ol_ref[...], wv_ref[...], preferred_element_type=F32).astype(BF16)
    x1 = (x_ref[...]
          + jnp.dot(mla, wout_ref[0:half, :], preferred_element_type=F32)
          + jnp.dot(cv_ref[...], wout_ref[half:, :], preferred_element_type=F32))
    x1_ref[...] = x1
    hq = _rms(x1, cg_ref[...]).astype(BF16)
    qc_ref[...] = (jnp.dot(hq, wcq_ref[...], preferred_element_type=F32) * MEM_SCALE).astype(BF16)


def _mix(x, olat, convo, wvbd, wout, cg, wcq):
    n = x.shape[0]
    row = lambda w: pl.BlockSpec((TM, w), lambda i: (i, 0))
    return pl.pallas_call(
        _mix_body,
        grid=(n // TM,),
        in_specs=[row(D_MODEL), row(N_HEADS * KV_LORA), row(CONV_DIM),
                  _const_spec((N_HEADS * KV_LORA, N_HEADS * V_DIM)),
                  _const_spec((D_MODEL, D_MODEL)), _const_spec((1, D_MODEL)),
                  _const_spec((D_MODEL, D_MODEL))],
        out_specs=[row(D_MODEL), row(D_MODEL)],
        out_shape=[jax.ShapeDtypeStruct((n, D_MODEL), F32),
                   jax.ShapeDtypeStruct((n, D_MODEL), BF16)],
        compiler_params=_params(("parallel",)),
        name="mix_out",
    )(x, olat, convo, wvbd, wout, cg, wcq)


def _memkv_body(mem_ref, g_ref, wk_ref, wv_ref, k_ref, v_ref):
    mn = _rms(mem_ref[...], g_ref[...]).astype(BF16)
    k_ref[...] = jnp.dot(mn, wk_ref[...], preferred_element_type=F32)
    v_ref[...] = jnp.dot(mn, wv_ref[...], preferred_element_type=F32)


def _memkv(mem, g, wk, wv):
    n = mem.shape[0]
    row = pl.BlockSpec((N_MEM, D_MODEL), lambda i: (i, 0))
    return pl.pallas_call(
        _memkv_body,
        grid=(n // N_MEM,),
        in_specs=[row, _const_spec((1, D_MODEL)), _const_spec((D_MODEL, D_MODEL)),
                  _const_spec((D_MODEL, D_MODEL))],
        out_specs=[row, row],
        out_shape=[jax.ShapeDtypeStruct((n, D_MODEL), F32)] * 2,
        compiler_params=_params(("parallel",)),
        name="memory_kv",
    )(mem, g, wk, wv)


def _cross_head(q, k, v):
    s = lax.dot_general(q, k.astype(BF16), (((1,), (1,)), ((), ())), preferred_element_type=F32)
    e = jnp.exp(s - jnp.max(s, axis=1, keepdims=True))
    p = (e * (1.0 / jnp.sum(e, axis=1, keepdims=True))).astype(BF16)
    return jnp.dot(p, v.astype(BF16), preferred_element_type=F32).astype(BF16)


def _cross_p_body(q_ref, k_ref, v_ref, o_ref):
    for hh in range(MEM_HEADS):
        sl = slice(hh * MEM_HEAD_DIM, (hh + 1) * MEM_HEAD_DIM)
        o_ref[:, sl] = _cross_head(q_ref[:, sl], k_ref[:, sl], v_ref[:, sl])


def _cross_prompt(qc, mk, mv, batch, seq):
    nt = seq // TM
    kv = pl.BlockSpec((N_MEM, D_MODEL), lambda b, i: (b, 0))
    row = pl.BlockSpec((TM, D_MODEL), lambda b, i: (b * nt + i, 0))
    return pl.pallas_call(
        _cross_p_body,
        grid=(batch, nt),
        in_specs=[row, kv, kv],
        out_specs=row,
        out_shape=jax.ShapeDtypeStruct((batch * seq, D_MODEL), BF16),
        compiler_params=_params(("parallel", "arbitrary")),
        name="cross_prompt",
    )(qc, mk, mv)


def _cross_s_body(q_ref, k_ref, v_ref, o_ref):
    q = q_ref[...]
    kv = k_ref[...].astype(BF16)
    vv = v_ref[...].astype(BF16)
    s2 = lax.dot_general(q, kv, (((1,), (1,)), ((), ())), preferred_element_type=F32)
    nr, nc = s2.shape
    half = nr // 2
    r = lax.broadcasted_iota(jnp.int32, (nr, nc), 0)
    c = lax.broadcasted_iota(jnp.int32, (nr, nc), 1)
    same = jnp.logical_and(r // half == (c // MEM_HEADS) % 2,
                           (r // CROSS_TD) % MEM_HEADS == c % MEM_HEADS)
    part = jnp.where(same, s2, 0.0)
    both = part[0:half] + part[half:nr]
    r2 = lax.broadcasted_iota(jnp.int32, (half, nc), 0)
    c2 = lax.broadcasted_iota(jnp.int32, (half, nc), 1)
    cj = (c2 // MEM_HEADS) % 2
    head_ok = c2 % MEM_HEADS == (r2 // CROSS_TD) % MEM_HEADS
    lo = jnp.where(cj == 0, both, 0.0)
    hi = jnp.where(cj == 1, both, 0.0)
    score = both + pltpu.roll(lo, MEM_HEADS, 1) + pltpu.roll(hi, nc - MEM_HEADS, 1)
    sv = jnp.where(head_ok, score, NEG_INF)
    e = jnp.exp(sv - jnp.max(sv, axis=1, keepdims=True))
    p = e * (2.0 / jnp.sum(e, axis=1, keepdims=True))
    pexp = jnp.concatenate([jnp.where(cj == 0, p, 0.0), jnp.where(cj == 1, p, 0.0)],
                           axis=0).astype(BF16)
    o_ref[...] = jnp.dot(pexp, vv, preferred_element_type=F32).astype(BF16)


def _cross_sample(qc, mem_k, mem_v, layer):
    bd = mem_k.shape[1]
    td = qc.shape[0] // bd
    assert td == CROSS_TD and MEM_HEAD_DIM == 2 * 128
    rows = 2 * MEM_HEADS * td
    kv_rows = N_MEM * 2 * MEM_HEADS

    def stored_order(x):
        x = x[layer].reshape(bd, N_MEM, MEM_HEADS, 2, 128)
        return jnp.transpose(x, (0, 1, 3, 2, 4)).reshape(bd, kv_rows, 128)

    q2 = jnp.transpose(qc.reshape(bd, td, MEM_HEADS, 2, 128), (0, 3, 2, 1, 4)).reshape(bd, rows, 128)
    kv = pl.BlockSpec((None, kv_rows, 128), lambda b: (b, 0, 0))
    row = pl.BlockSpec((None, rows, 128), lambda b: (b, 0, 0))
    o2 = pl.pallas_call(
        _cross_s_body,
        grid=(bd,),
        in_specs=[row, kv, kv],
        out_specs=row,
        out_shape=jax.ShapeDtypeStruct((bd, rows, 128), BF16),
        compiler_params=_params(("parallel",)),
        name="cross_sample",
    )(q2, stored_order(mem_k), stored_order(mem_v))
    o = jnp.transpose(o2.reshape(bd, 2, MEM_HEADS, td, 128), (0, 3, 2, 1, 4))
    return o.reshape(bd * td, D_MODEL)


def _post_body(x1_ref, o_ref, wco_ref, g_ref, wg_ref, wu_ref, wd_ref, fn_ref, y_ref):
    x2 = x1_ref[...] + jnp.dot(o_ref[...], wco_ref[...], preferred_element_type=F32)
    h = _rms(x2, g_ref[...]).astype(BF16)
    x3 = x2 + 0.5 * _swiglu(h, wg_ref, wu_ref, wd_ref)
    y_ref[...] = _rms(x3, fn_ref[...])


def _post(x1, o, wco, g, wg, wu, wd, fn):
    n = x1.shape[0]
    row = pl.BlockSpec((TM, D_MODEL), lambda i: (i, 0))
    return pl.pallas_call(
        _post_body,
        grid=(n // TM,),
        in_specs=[row, row, _const_spec((D_MODEL, D_MODEL)), _const_spec((1, D_MODEL)),
                  _const_spec((D_MODEL, D_FF)), _const_spec((D_MODEL, D_FF)),
                  _const_spec((D_FF, D_MODEL)), _const_spec((1, D_MODEL))],
        out_specs=row,
        out_shape=jax.ShapeDtypeStruct((n, D_MODEL), F32),
        compiler_params=_params(("parallel",)),
        name="cross_out_ffn2",
    )(x1, o, wco, g, wg, wu, wd, fn)


def _rope_tables(pos):
    half = QK_ROPE // 2
    inv = ROPE_BASE ** (-jnp.arange(half, dtype=F32) / half)
    ang = pos.astype(F32)[:, None] * inv[None, :]
    cos = jnp.cos(ang)
    sin = jnp.sin(ang)
    cos_t = jnp.tile(jnp.concatenate([cos, cos], axis=1), (1, N_HEADS))
    sin_t = jnp.tile(jnp.concatenate([-sin, sin], axis=1), (1, N_HEADS))
    return cos_t, sin_t


def _swap_halves(w):
    half = QK_ROPE // 2
    return jnp.concatenate([w[..., half:], w[..., :half]], axis=-1)


def _layer_weights(l, ffn1_norm, ffn1_w_gate, ffn1_w_up, ffn1_w_down, mix_norm, w_in, q_norm,
                   w_q_up, kv_norm, w_k_up, w_v_up, conv_w, conv_b, conv_ln_g, conv_ln_b, w_out,
                   cross_norm, mem_norm, w_cq, w_mk, w_mv, w_co, ffn2_norm, ffn2_w_gate,
                   ffn2_w_up, ffn2_w_down):
    row = lambda v: v[l][None, :]
    wi = w_in[l]
    o1, o2, o3 = Q_LORA, Q_LORA + KV_LORA, Q_LORA + KV_LORA + QK_ROPE
    w_kr = wi[:, o2:o3]
    rep = KR_TILE // QK_ROPE
    win = jnp.concatenate(
        [wi[:, :o1], wi[:, o1:o2], wi[:, o3:o3 + CONV_DIM], wi[:, o3 + CONV_DIM:],
         jnp.tile(w_kr, (1, rep)), jnp.tile(_swap_halves(w_kr), (1, rep))], axis=1).astype(BF16)
    wq3 = w_q_up[l].reshape(Q_LORA, N_HEADS, QK_NOPE + QK_ROPE)
    wq_nope = jnp.pad(wq3[:, :, :QK_NOPE], ((0, 0), (0, 0), (0, NOPE_PAD - QK_NOPE)))
    wq_rope = wq3[:, :, QK_NOPE:]
    wq = jnp.concatenate(
        [wq_nope.reshape(Q_LORA, N_HEADS * NOPE_PAD), wq_rope.reshape(Q_LORA, ROPE_ALL),
         _swap_halves(wq_rope).reshape(Q_LORA, ROPE_ALL)], axis=1).astype(BF16)
    wk = jnp.pad(jnp.transpose(w_k_up[l], (1, 2, 0)),
                 ((0, 0), (0, NOPE_PAD - QK_NOPE), (0, 0))).astype(BF16)
    wv = w_v_up[l]
    eye = jnp.eye(N_HEADS, dtype=wv.dtype)
    wvbd = jnp.einsum('chv,hg->hcgv', wv, eye).reshape(
        N_HEADS * KV_LORA, N_HEADS * V_DIM).astype(BF16)
    w8 = jnp.broadcast_to(conv_w[l][:, None, :], (CONV_WIDTH, 8, CONV_DIM))
    return dict(
        ffn1=(row(ffn1_norm), ffn1_w_gate[l].astype(BF16), ffn1_w_up[l].astype(BF16),
              ffn1_w_down[l].astype(BF16)),
        proj=(row(mix_norm), win, row(q_norm), wq, row(kv_norm), wk),
        conv=(w8, row(conv_b), row(conv_ln_g), row(conv_ln_b)),
        mix=(wvbd, w_out[l].astype(BF16), row(cross_norm), w_cq[l].astype(BF16)),
        mem=(row(mem_norm), w_mk[l].astype(BF16), w_mv[l].astype(BF16)),
        post=(w_co[l].astype(BF16), row(ffn2_norm), ffn2_w_gate[l].astype(BF16),
              ffn2_w_up[l].astype(BF16), ffn2_w_down[l].astype(BF16)),
    )


def kernel(x_prompt, x_sample, mem_prompt, cache_kv_latent, cache_k_rope, state_conv, cache_mem_k, cache_mem_v, page_table, ffn1_norm, ffn1_w_gate, ffn1_w_up, ffn1_w_down, mix_norm, w_in, q_norm, w_q_up, kv_norm, w_k_up, w_v_up, conv_w, conv_b, conv_ln_g, conv_ln_b, w_out, cross_norm, mem_norm, w_cq, w_mk, w_mv, w_co, ffn2_norm, ffn2_w_gate, ffn2_w_up, ffn2_w_down, final_norm):
    batch, seq, _ = x_prompt.shape
    bd, td, _ = x_sample.shape
    depth = ffn1_norm.shape[0]
    assert depth == 1, "the final norm is fused into the last layer's kernel"
    past_len = page_table.shape[1] * PAGE_SIZE
    n_p, n_s = batch * seq, bd * td
    state_w = CONV_WIDTH - 1

    cos_p, sin_p = _rope_tables(jnp.arange(seq, dtype=jnp.int32))
    cos_s, sin_s = _rope_tables(past_len + jnp.arange(td, dtype=jnp.int32))
    cos_s, sin_s = jnp.tile(cos_s, (TM // td, 1)), jnp.tile(sin_s, (TM // td, 1))

    xp = x_prompt.reshape(n_p, D_MODEL)
    xs = x_sample.reshape(n_s, D_MODEL)
    fn = final_norm[None, :]
    outs = {k: [] for k in ("kvl_p", "kr_p", "cs_p", "mk_p", "mv_p", "kvl_s", "kr_s", "cs_s")}
    for l in range(depth):
        w = _layer_weights(l, ffn1_norm, ffn1_w_gate, ffn1_w_up, ffn1_w_down, mix_norm, w_in,
                           q_norm, w_q_up, kv_norm, w_k_up, w_v_up, conv_w, conv_b, conv_ln_g,
                           conv_ln_b, w_out, cross_norm, mem_norm, w_cq, w_mk, w_mv, w_co,
                           ffn2_norm, ffn2_w_gate, ffn2_w_up, ffn2_w_down)
        xp = _ffn(xp, *w["ffn1"])
        xs = _ffn(xs, *w["ffn1"])

        qlat, qrope, kfull, ckv, krope, u = _proj(xp, cos_p, sin_p, seq // TM, *w["proj"])
        olat = _attn_prompt(qlat, qrope, kfull, batch, seq)
        convo = _conv_prompt(u, *w["conv"], batch, seq)
        xp1, qc_p = _mix(xp, olat, convo, *w["mix"])
        outs["kvl_p"].append(ckv.reshape(batch, seq, KV_LORA))
        outs["kr_p"].append(krope.reshape(batch, seq, QK_ROPE))
        outs["cs_p"].append(u.reshape(batch, seq, CONV_DIM)[:, seq - state_w:])

        qlat, qrope, _, ckv, krope, u = _proj(xs, cos_s, sin_s, 1, *w["proj"])
        rows = td * N_HEADS
        olat = _attn_decode(
            page_table,
            qlat.reshape(bd, rows, KV_LORA), qrope.reshape(bd, rows, QK_ROPE),
            ckv.reshape(bd, td, KV_LORA), krope.reshape(bd, td, QK_ROPE),
            cache_kv_latent, jnp.swapaxes(cache_k_rope, 2, 3), l).reshape(n_s, N_HEADS * KV_LORA)
        u3 = u.reshape(bd, td, CONV_DIM)
        ext = jnp.concatenate([state_conv[l], u3], axis=1)
        convo = _conv_sample(jnp.transpose(ext, (1, 0, 2)), *w["conv"])
        convo = jnp.transpose(convo, (1, 0, 2)).reshape(n_s, CONV_DIM)
        xs1, qc_s = _mix(xs, olat, convo, *w["mix"])
        outs["kvl_s"].append(ckv.reshape(bd, td, KV_LORA))
        outs["kr_s"].append(krope.reshape(bd, td, QK_ROPE))
        outs["cs_s"].append(ext[:, td:])

        mk, mv = _memkv(mem_prompt.reshape(batch * N_MEM, D_MODEL), *w["mem"])
        o_p = _cross_prompt(qc_p, mk, mv, batch, seq)
        o_s = _cross_sample(qc_s, cache_mem_k, cache_mem_v, l)
        outs["mk_p"].append(mk.reshape(batch, N_MEM, MEM_HEADS, MEM_HEAD_DIM))
        outs["mv_p"].append(mv.reshape(batch, N_MEM, MEM_HEADS, MEM_HEAD_DIM))

        xp = _post(xp1, o_p, *w["post"], fn)
        xs = _post(xs1, o_s, *w["post"], fn)

    st = lambda k: jnp.stack(outs[k])
    return (xp.reshape(batch, seq, D_MODEL), xs.reshape(bd, td, D_MODEL),
            st("kvl_p"), st("kr_p"), st("cs_p"), st("mk_p"), st("mv_p"),
            st("kvl_s"), st("kr_s"), st("cs_s"))
```

```python
import functools

import jax
import jax.numpy as jnp
from jax import lax
from jax.experimental import pallas as pl
from jax.experimental.pallas import tpu as pltpu

F32 = jnp.float32
BF16 = jnp.bfloat16

D_MODEL = 1024
N_HEADS = 8
QK_NOPE = 64
QK_ROPE = 32
V_DIM = 64
Q_LORA = 384
KV_LORA = 256
CONV_DIM = 512
CONV_WIDTH = 31
D_FF = 2816
N_MEM = 256
MEM_HEADS = 4
MEM_HEAD_DIM = D_MODEL // MEM_HEADS
PAGE_SIZE = 128
ROPE_BASE = 10000.0
EPS = 1e-6
NEG_INF = -1e30
ATTN_SCALE = (QK_NOPE + QK_ROPE) ** -0.5
MEM_SCALE = MEM_HEAD_DIM ** -0.5
QK_LOG2_SCALE = ATTN_SCALE * 1.4426950408889634

VMEM_LIMIT = 56 * 1024 * 1024

TM = 512
NOPE_PAD = 128
ROPE_ALL = N_HEADS * QK_ROPE
KR_TILE = 128
K_FULL = KV_LORA + KR_TILE
O_CQ, O_CKV, O_A, O_GT, O_KR, O_KRS, W_IN_COLS = 0, 384, 640, 1152, 1664, 1792, 1920
O_QN, O_QR, O_QRS, W_Q_COLS = 0, 1024, 1280, 1536

BQ = 256
BK = 512
ATTN_GROUP_HEADS = 2
PAGES_PER_STEP = 64
DECODE_SUBCHUNKS = 4
CROSS_TD = 4
CROSS_BB = 2
CONV_TT = 256
CONV_R = 32
CONV_HALO = 32


def _const_spec(shape):
    nd = len(shape)
    return pl.BlockSpec(shape, lambda *_: (0,) * nd, pipeline_mode=pl.Buffered(1))


def _params(sem):
    return pltpu.CompilerParams(dimension_semantics=sem, vmem_limit_bytes=VMEM_LIMIT)


def _rms(x, g):
    ms = jnp.mean(x * x, axis=-1, keepdims=True)
    return x * lax.rsqrt(ms + EPS) * g


def _swiglu(h, wg_ref, wu_ref, wd_ref):
    gate = jnp.dot(h, wg_ref[...], preferred_element_type=F32)
    up = jnp.dot(h, wu_ref[...], preferred_element_type=F32)
    a = (gate * jax.nn.sigmoid(gate) * up).astype(BF16)
    return jnp.dot(a, wd_ref[...], preferred_element_type=F32)


def _ffn_body(x_ref, g_ref, wg_ref, wu_ref, wd_ref, o_ref):
    x = x_ref[...]
    h = _rms(x, g_ref[...]).astype(BF16)
    o_ref[...] = x + 0.5 * _swiglu(h, wg_ref, wu_ref, wd_ref)


def _ffn(x, g, wg, wu, wd):
    n = x.shape[0]
    row = pl.BlockSpec((TM, D_MODEL), lambda i: (i, 0))
    return pl.pallas_call(
        _ffn_body,
        grid=(n // TM,),
        in_specs=[row, _const_spec((1, D_MODEL)), _const_spec((D_MODEL, D_FF)),
                  _const_spec((D_MODEL, D_FF)), _const_spec((D_FF, D_MODEL))],
        out_specs=row,
        out_shape=jax.ShapeDtypeStruct((n, D_MODEL), F32),
        compiler_params=_params(("parallel",)),
        name="ffn1",
    )(x, g, wg, wu, wd)


def _proj_body(x_ref, cos_ref, sin_ref, g_ref, win_ref, qn_ref, wq_ref, kvn_ref, wk_ref,
               qlat_ref, qrope_ref, kfull_ref, ckv_ref, krope_ref, u_ref):
    h = _rms(x_ref[...], g_ref[...]).astype(BF16)
    z = jnp.dot(h, win_ref[...], preferred_element_type=F32)
    cos = cos_ref[...]
    sin = sin_ref[...]
    u_ref[...] = z[:, O_A:O_GT] * jax.nn.sigmoid(z[:, O_GT:O_KR])
    c = _rms(z[:, O_CKV:O_A], kvn_ref[...])
    ckv_ref[...] = c
    kr = z[:, O_KR:O_KRS] * cos[:, :KR_TILE] + z[:, O_KRS:W_IN_COLS] * sin[:, :KR_TILE]
    krope_ref[...] = kr[:, :QK_ROPE]
    kfull_ref[:, 0:KV_LORA] = c.astype(BF16)
    kfull_ref[:, KV_LORA:K_FULL] = kr.astype(BF16)
    cqn = _rms(z[:, O_CQ:O_CKV], qn_ref[...]).astype(BF16)
    q = jnp.dot(cqn, wq_ref[...], preferred_element_type=F32)
    qrope_ref[...] = ((q[:, O_QR:O_QRS] * cos + q[:, O_QRS:W_Q_COLS] * sin)
                      * QK_LOG2_SCALE).astype(BF16)
    for hh in range(N_HEADS):
        qn = q[:, hh * NOPE_PAD:(hh + 1) * NOPE_PAD].astype(BF16)
        qlat_ref[:, hh * KV_LORA:(hh + 1) * KV_LORA] = (jnp.dot(
            qn, wk_ref[hh], preferred_element_type=F32) * QK_LOG2_SCALE).astype(BF16)


def _proj(x, cos, sin, tab_blocks, g, win, qn, wq, kvn, wk):
    n = x.shape[0]
    row = lambda w: pl.BlockSpec((TM, w), lambda i: (i, 0))
    tab = pl.BlockSpec((TM, ROPE_ALL), lambda i: (i % tab_blocks, 0))
    return pl.pallas_call(
        _proj_body,
        grid=(n // TM,),
        in_specs=[row(D_MODEL), tab, tab, _const_spec((1, D_MODEL)),
                  _const_spec((D_MODEL, W_IN_COLS)), _const_spec((1, Q_LORA)),
                  _const_spec((Q_LORA, W_Q_COLS)), _const_spec((1, KV_LORA)),
                  _const_spec((N_HEADS, NOPE_PAD, KV_LORA))],
        out_specs=[row(N_HEADS * KV_LORA), row(ROPE_ALL), row(K_FULL), row(KV_LORA),
                   row(QK_ROPE), row(CONV_DIM)],
        out_shape=[jax.ShapeDtypeStruct((n, N_HEADS * KV_LORA), BF16),
                   jax.ShapeDtypeStruct((n, ROPE_ALL), BF16),
                   jax.ShapeDtypeStruct((n, K_FULL), BF16),
                   jax.ShapeDtypeStruct((n, KV_LORA), F32),
                   jax.ShapeDtypeStruct((n, QK_ROPE), F32),
                   jax.ShapeDtypeStruct((n, CONV_DIM), F32)],
        compiler_params=_params(("parallel",)),
        name="mixer_proj",
    )(x, cos, sin, g, win, qn, wq, kvn, wk)


def _lane_tile(v, n):
    return jnp.concatenate([v] * n, axis=1) if n > 1 else v


def _attn_p_body(qlat_ref, qrope_ref, k_ref, o_ref, qs_ref, m_ref, l_ref, acc_ref):
    i = pl.program_id(1)
    lane = lax.broadcasted_iota(jnp.int32, (BQ, KR_TILE), 1)
    heads_per_tile = KR_TILE // QK_ROPE
    for hh in range(N_HEADS):
        r0 = hh * BQ
        qs_ref[r0:r0 + BQ, 0:KV_LORA] = qlat_ref[:, hh * KV_LORA:(hh + 1) * KV_LORA]
        t = hh // heads_per_tile
        grp = qrope_ref[:, t * KR_TILE:(t + 1) * KR_TILE]
        lo = (hh % heads_per_tile) * QK_ROPE
        keep = jnp.logical_and(lane >= lo, lane < lo + QK_ROPE)
        qs_ref[r0:r0 + BQ, KV_LORA:K_FULL] = jnp.where(keep, grp, jnp.zeros_like(grp))
    m_ref[...] = jnp.full(m_ref.shape, NEG_INF, F32)
    l_ref[...] = jnp.zeros(l_ref.shape, F32)
    acc_ref[...] = jnp.zeros(acc_ref.shape, F32)
    q0 = i * BQ
    n_full = q0 // BK
    n_groups = N_HEADS // ATTN_GROUP_HEADS
    g_rows = ATTN_GROUP_HEADS * BQ

    def part_a(kb, masked):
        start = pl.multiple_of(kb * BK, BK)
        k = k_ref[pl.ds(start, BK), :]
        ps, alphas = [], []
        for g in range(n_groups):
            rs = slice(g * g_rows, (g + 1) * g_rows)
            s = lax.dot_general(qs_ref[rs, :], k, (((1,), (1,)), ((), ())),
                                preferred_element_type=F32)
            if masked:
                r = lax.broadcasted_iota(jnp.int32, (g_rows, BK), 0)
                col = lax.broadcasted_iota(jnp.int32, (g_rows, BK), 1)
                qpos = q0 + jnp.bitwise_and(r, BQ - 1)
                s = jnp.where(start + col <= qpos, s, NEG_INF)
            m_prev = m_ref[rs, :]
            m_next = jnp.maximum(m_prev, jnp.max(s, axis=1, keepdims=True))
            alpha = jnp.exp2(m_prev - m_next)
            p = jnp.exp2(s - _lane_tile(m_next, BK // 128))
            l_ref[rs, :] = alpha * l_ref[rs, :] + jnp.sum(p, axis=1, keepdims=True)
            m_ref[rs, :] = m_next
            ps.append(p.astype(BF16))
            alphas.append(alpha)
        return tuple(ps), tuple(alphas)

    def part_b(kb, ps, alphas):
        start = pl.multiple_of(kb * BK, BK)
        v = k_ref[pl.ds(start, BK), 0:KV_LORA]
        for g in range(n_groups):
            rs = slice(g * g_rows, (g + 1) * g_rows)
            acc_ref[rs, :] = (acc_ref[rs, :] * _lane_tile(alphas[g], KV_LORA // 128)
                              + jnp.dot(ps[g], v, preferred_element_type=F32))

    def trip(kb, carry):
        ps, alphas = part_a(kb, False)
        part_b(jnp.maximum(kb - 1, 0), *carry)
        return ps, alphas

    nothing = (tuple(jnp.zeros((g_rows, BK), BF16) for _ in range(n_groups)),
               tuple(jnp.ones((g_rows, 128), F32) for _ in range(n_groups)))
    lagging = lax.fori_loop(0, n_full, trip, nothing)
    ps, alphas = part_a(n_full, True)
    part_b(jnp.maximum(n_full - 1, 0), *lagging)
    part_b(n_full, ps, alphas)
    inv = 1.0 / l_ref[...]
    for hh in range(N_HEADS):
        r0 = hh * BQ
        o_ref[:, hh * KV_LORA:(hh + 1) * KV_LORA] = (
            acc_ref[r0:r0 + BQ, :] * _lane_tile(inv[r0:r0 + BQ, :], KV_LORA // 128)).astype(BF16)


def _attn_prompt(qlat, qrope, kfull, batch, seq):
    nq = seq // BQ
    rows = N_HEADS * BQ
    return pl.pallas_call(
        _attn_p_body,
        grid=(batch, nq),
        in_specs=[pl.BlockSpec((BQ, N_HEADS * KV_LORA), lambda b, i: (b * nq + i, 0)),
                  pl.BlockSpec((BQ, ROPE_ALL), lambda b, i: (b * nq + i, 0)),
                  pl.BlockSpec((seq, K_FULL), lambda b, i: (b, 0))],
        out_specs=pl.BlockSpec((BQ, N_HEADS * KV_LORA), lambda b, i: (b * nq + i, 0)),
        out_shape=jax.ShapeDtypeStruct((batch * seq, N_HEADS * KV_LORA), BF16),
        scratch_shapes=[pltpu.VMEM((rows, K_FULL), BF16), pltpu.VMEM((rows, 128), F32),
                        pltpu.VMEM((rows, 128), F32), pltpu.VMEM((rows, KV_LORA), F32)],
        compiler_params=_params(("parallel", "arbitrary")),
        name="attn_prompt",
    )(qlat, qrope, kfull)


def _attn_d_body(pt_ref, qlat_ref, qr_ref, cnew_ref, krnew_ref, cc_hbm, ckr_hbm, o_ref,
                 cbuf, krb, m_ref, l_ref, acc_ref, sem_c, sem_k, *, layer):
    g_pages = cbuf.shape[1]
    b = pl.program_id(0)
    j = pl.program_id(1)
    nj = pl.num_programs(1)
    n_steps = pl.num_programs(0) * nj
    t = b * nj + j
    slot = lax.rem(t, 2)
    rows = qlat_ref.shape[0]
    td = rows // N_HEADS

    def page_copies(step, sl, g):
        page = pt_ref[step * g_pages + g]
        return (pltpu.make_async_copy(cc_hbm.at[layer, page], cbuf.at[sl, g], sem_c.at[sl]),
                pltpu.make_async_copy(ckr_hbm.at[layer, page], krb.at[sl, g], sem_k.at[sl]))

    def start_step(step, sl):
        def issue(g, carry):
            for cp in page_copies(step, sl, g):
                cp.start()
            return carry
        lax.fori_loop(0, g_pages, issue, 0, unroll=8)

    @pl.when(t == 0)
    def _():
        start_step(0, 0)

    @pl.when(t + 1 < n_steps)
    def _():
        start_step(t + 1, 1 - slot)

    def wait_page(g, carry):
        for cp in page_copies(t, slot, g):
            cp.wait()
        return carry
    lax.fori_loop(0, g_pages, wait_page, 0, unroll=8)

    @pl.when(j == 0)
    def _():
        m_ref[...] = jnp.full(m_ref.shape, NEG_INF, F32)
        l_ref[...] = jnp.zeros(l_ref.shape, F32)
        acc_ref[...] = jnp.zeros(acc_ref.shape, F32)

    q = qlat_ref[...].astype(F32)
    qr = qr_ref[...].astype(F32)
    sub_pages = g_pages // DECODE_SUBCHUNKS
    sub = sub_pages * PAGE_SIZE
    keys, scores = [], []
    for h in range(DECODE_SUBCHUNKS):
        p0 = h * sub_pages
        ks = cbuf[slot, p0:p0 + sub_pages].reshape(sub, KV_LORA)
        krs = jnp.concatenate([krb[slot, p0 + g] for g in range(sub_pages)], axis=1)
        keys.append(ks)
        scores.append(lax.dot_general(q, ks, (((1,), (1,)), ((), ())), preferred_element_type=F32)
                      + jnp.dot(qr, krs, preferred_element_type=F32))
    m_next = m_ref[...]
    l_new = l_ref[...]
    acc = acc_ref[...]
    for ks, s in zip(keys, scores):
        m_prev = m_next
        m_next = jnp.maximum(m_prev, jnp.max(s, axis=1, keepdims=True))
        alpha = jnp.exp2(m_prev - m_next)
        p = jnp.exp2(s - _lane_tile(m_next, sub // 128))
        l_new = alpha * l_new + jnp.sum(p, axis=1, keepdims=True)
        acc = (acc * _lane_tile(alpha, KV_LORA // 128)
               + jnp.dot(p.astype(BF16).astype(F32), ks, preferred_element_type=F32))
    l_ref[...] = l_new
    m_ref[...] = m_next
    acc_ref[...] = acc

    @pl.when(j == nj - 1)
    def _():
        qf = q
        qrf = qr
        tok = lax.broadcasted_iota(jnp.int32, (rows, 1), 0) // N_HEADS
        s_new = []
        for t2 in range(td):
            cn = cnew_ref[t2:t2 + 1, :].astype(BF16).astype(F32)
            kn = krnew_ref[t2:t2 + 1, :].astype(BF16).astype(F32)
            st = jnp.sum(qf * cn, axis=1, keepdims=True) + jnp.sum(qrf * kn, axis=1, keepdims=True)
            s_new.append(jnp.where(t2 <= tok, st, NEG_INF))
        m_prev = m_ref[...]
        m_cur = s_new[0]
        for st in s_new[1:]:
            m_cur = jnp.maximum(m_cur, st)
        m_fin = jnp.maximum(m_prev, m_cur)
        alpha = jnp.exp2(m_prev - m_fin)
        l_fin = alpha * l_ref[...]
        acc_fin = acc_ref[...] * _lane_tile(alpha, KV_LORA // 128)
        for t2 in range(td):
            pt = jnp.exp2(s_new[t2] - m_fin)
            l_fin = l_fin + pt
            cn = cnew_ref[t2:t2 + 1, :].astype(BF16).astype(F32)
            acc_fin = acc_fin + _lane_tile(pt.astype(BF16).astype(F32), KV_LORA // 128) * cn
        o_ref[...] = (acc_fin * _lane_tile(1.0 / l_fin, KV_LORA // 128)).astype(BF16)


def _attn_decode(page_table, qlat, qrope, cnew, krnew, cache_c, cache_kr_t, layer):
    bd, rows, _ = qlat.shape
    td = rows // N_HEADS
    n_pages = page_table.shape[1]
    g_pages = PAGES_PER_STEP
    nj = n_pages // g_pages
    per_b = lambda r, w: pl.BlockSpec((None, r, w), lambda b, j, pt: (b, 0, 0))
    hbm = pl.BlockSpec(memory_space=pl.ANY)
    grid_spec = pltpu.PrefetchScalarGridSpec(
        num_scalar_prefetch=1,
        grid=(bd, nj),
        in_specs=[per_b(rows, KV_LORA), per_b(rows, QK_ROPE), per_b(td, KV_LORA),
                  per_b(td, QK_ROPE), hbm, hbm],
        out_specs=per_b(rows, KV_LORA),
        scratch_shapes=[pltpu.VMEM((2, g_pages, PAGE_SIZE, KV_LORA), F32),
                        pltpu.VMEM((2, g_pages, QK_ROPE, PAGE_SIZE), F32),
                        pltpu.VMEM((rows, 128), F32), pltpu.VMEM((rows, 128), F32),
                        pltpu.VMEM((rows, KV_LORA), F32),
                        pltpu.SemaphoreType.DMA((2,)), pltpu.SemaphoreType.DMA((2,))])
    return pl.pallas_call(
        functools.partial(_attn_d_body, layer=layer),
        grid_spec=grid_spec,
        out_shape=jax.ShapeDtypeStruct((bd, rows, KV_LORA), BF16),
        compiler_params=_params(("arbitrary", "arbitrary")),
        name="attn_decode",
    )(page_table.reshape(-1), qlat, qrope, cnew, krnew, cache_c, cache_kr_t)


def _ln_silu(y, g, b):
    mu = jnp.mean(y, axis=-1, keepdims=True)
    d = y - mu
    var = jnp.mean(d * d, axis=-1, keepdims=True)
    z = d * lax.rsqrt(var + EPS) * g + b
    return z * jax.nn.sigmoid(z)


def _conv_p_body(prev_ref, cur_ref, w_ref, b_ref, g_ref, beta_ref, o_ref, xb_ref):
    i = pl.program_id(1)
    xb_ref[0:CONV_HALO, :] = jnp.where(i > 0, prev_ref[...], 0.0)
    xb_ref[CONV_HALO:CONV_HALO + CONV_TT, :] = cur_ref[...]
    off = CONV_HALO - (CONV_WIDTH - 1)
    for r in range(CONV_TT // CONV_R):
        acc = jnp.zeros((CONV_R // 8, 8, CONV_DIM), F32)
        for k in range(CONV_WIDTH):
            lo = r * CONV_R + off + k
            xs = xb_ref[lo:lo + CONV_R, :].reshape(CONV_R // 8, 8, CONV_DIM)
            acc = acc + xs * w_ref[k][None]
        y = acc.reshape(CONV_R, CONV_DIM) + b_ref[...]
        o_ref[r * CONV_R:(r + 1) * CONV_R, :] = _ln_silu(y, g_ref[...], beta_ref[...]).astype(BF16)


def _conv_prompt(u, w8, b, g, beta, batch, seq):
    nt = seq // CONV_TT
    halo_per_tile = CONV_TT // CONV_HALO
    halo_per_seq = seq // CONV_HALO
    return pl.pallas_call(
        _conv_p_body,
        grid=(batch, nt),
        in_specs=[pl.BlockSpec((CONV_HALO, CONV_DIM),
                               lambda bb, i: (bb * halo_per_seq + jnp.maximum(i * halo_per_tile - 1, 0), 0)),
                  pl.BlockSpec((CONV_TT, CONV_DIM), lambda bb, i: (bb * nt + i, 0)),
                  _const_spec((CONV_WIDTH, 8, CONV_DIM)), _const_spec((1, CONV_DIM)),
                  _const_spec((1, CONV_DIM)), _const_spec((1, CONV_DIM))],
        out_specs=pl.BlockSpec((CONV_TT, CONV_DIM), lambda bb, i: (bb * nt + i, 0)),
        out_shape=jax.ShapeDtypeStruct((batch * seq, CONV_DIM), BF16),
        scratch_shapes=[pltpu.VMEM((CONV_HALO + CONV_TT, CONV_DIM), F32)],
        compiler_params=_params(("parallel", "arbitrary")),
        name="conv_prompt",
    )(u, u, w8, b, g, beta)


def _conv_s_body(ext_ref, w_ref, b_ref, g_ref, beta_ref, o_ref):
    td, bd, _ = o_ref.shape
    for rg in range(bd // CONV_R):
        for t in range(td):
            acc = jnp.zeros((CONV_R // 8, 8, CONV_DIM), F32)
            for k in range(CONV_WIDTH):
                xs = ext_ref[t + k, rg * CONV_R:(rg + 1) * CONV_R, :].reshape(CONV_R // 8, 8, CONV_DIM)
                acc = acc + xs * w_ref[k][None]
            y = acc.reshape(CONV_R, CONV_DIM) + b_ref[...]
            o_ref[t, rg * CONV_R:(rg + 1) * CONV_R, :] = _ln_silu(
                y, g_ref[...], beta_ref[...]).astype(BF16)


def _conv_sample(ext_t, w8, b, g, beta):
    n_ext, bd, _ = ext_t.shape
    td = n_ext - (CONV_WIDTH - 1)
    return pl.pallas_call(
        _conv_s_body,
        out_shape=jax.ShapeDtypeStruct((td, bd, CONV_DIM), BF16),
        compiler_params=pltpu.CompilerParams(vmem_limit_bytes=VMEM_LIMIT),
        name="conv_sample",
    )(ext_t, w8, b, g, beta)


def _mix_body(x_ref, ol_ref, cv_ref, wv_ref, wout_ref, cg_ref, wcq_ref, x1_ref, qc_ref):
    half = N_HEADS * V_DIM
    mla = jnp.dot(ol_ref[...], wv_ref[...], preferred_element_type=F32).astype(BF16)
    x1 = (x_ref[...]
          + jnp.dot(mla, wout_ref[0:half, :], preferred_element_type=F32)
          + jnp.dot(cv_ref[...], wout_ref[half:, :], preferred_element_type=F32))
    x1_ref[...] = x1
    hq = _rms(x1, cg_ref[...]).astype(BF16)
    qc_ref[...] = (jnp.dot(hq, wcq_ref[...], preferred_element_type=F32) * MEM_SCALE).astype(BF16)


def _mix(x, olat, convo, wvbd, wout, cg, wcq):
    n = x.shape[0]
    row = lambda w: pl.BlockSpec((TM, w), lambda i: (i, 0))
    return pl.pallas_call(
        _mix_body,
        grid=(n // TM,),
        in_specs=[row(D_MODEL), row(N_HEADS * KV_LORA), row(CONV_DIM),
                  _const_spec((N_HEADS * KV_LORA, N_HEADS * V_DIM)),
                  _const_spec((D_MODEL, D_MODEL)), _const_spec((1, D_MODEL)),
                  _const_spec((D_MODEL, D_MODEL))],
        out_specs=[row(D_MODEL), row(D_MODEL)],
        out_shape=[jax.ShapeDtypeStruct((n, D_MODEL), F32),
                   jax.ShapeDtypeStruct((n, D_MODEL), BF16)],
        compiler_params=_params(("parallel",)),
        name="mix_out",
    )(x, olat, convo, wvbd, wout, cg, wcq)


def _memkv_body(mem_ref, g_ref, wk_ref, wv_ref, k_ref, v_ref):
    mn = _rms(mem_ref[...], g_ref[...]).astype(BF16)
    k_ref[...] = jnp.dot(mn, wk_ref[...], preferred_element_type=F32)
    v_ref[...] = jnp.dot(mn, wv_ref[...], preferred_element_type=F32)


def _memkv(mem, g, wk, wv):
    n = mem.shape[0]
    row = pl.BlockSpec((N_MEM, D_MODEL), lambda i: (i, 0))
    return pl.pallas_call(
        _memkv_body,
        grid=(n // N_MEM,),
        in_specs=[row, _const_spec((1, D_MODEL)), _const_spec((D_MODEL, D_MODEL)),
                  _const_spec((D_MODEL, D_MODEL))],
        out_specs=[row, row],
        out_shape=[jax.ShapeDtypeStruct((n, D_MODEL), F32)] * 2,
        compiler_params=_params(("parallel",)),
        name="memory_kv",
    )(mem, g, wk, wv)


def _cross_head(q, k, v):
    s = lax.dot_general(q, k.astype(BF16), (((1,), (1,)), ((), ())), preferred_element_type=F32)
    e = jnp.exp(s - jnp.max(s, axis=1, keepdims=True))
    p = (e * (1.0 / jnp.sum(e, axis=1, keepdims=True))).astype(BF16)
    return jnp.dot(p, v.astype(BF16), preferred_element_type=F32).astype(BF16)


def _cross_p_body(q_ref, k_ref, v_ref, o_ref):
    for hh in range(MEM_HEADS):
        sl = slice(hh * MEM_HEAD_DIM, (hh + 1) * MEM_HEAD_DIM)
        o_ref[:, sl] = _cross_head(q_ref[:, sl], k_ref[:, sl], v_ref[:, sl])


def _cross_prompt(qc, mk, mv, batch, seq):
    nt = seq // TM
    kv = pl.BlockSpec((N_MEM, D_MODEL), lambda b, i: (b, 0))
    row = pl.BlockSpec((TM, D_MODEL), lambda b, i: (b * nt + i, 0))
    return pl.pallas_call(
        _cross_p_body,
        grid=(batch, nt),
        in_specs=[row, kv, kv],
        out_specs=row,
        out_shape=jax.ShapeDtypeStruct((batch * seq, D_MODEL), BF16),
        compiler_params=_params(("parallel", "arbitrary")),
        name="cross_prompt",
    )(qc, mk, mv)


def _cross_s_body(q_ref, k_ref, v_ref, o_ref):
    for i in range(q_ref.shape[0]):
        o_ref[i] = _cross_s_one(q_ref[i], k_ref[i], v_ref[i])


def _cross_s_one(q, k, v):
    kv = k.astype(BF16)
    vv = v.astype(BF16)
    s2 = lax.dot_general(q, kv, (((1,), (1,)), ((), ())), preferred_element_type=F32)
    nr, nc = s2.shape
    half = nr // 2
    r = lax.broadcasted_iota(jnp.int32, (nr, nc), 0)
    c = lax.broadcasted_iota(jnp.int32, (nr, nc), 1)
    same = jnp.logical_and(r // half == (c // MEM_HEADS) % 2,
                           (r // CROSS_TD) % MEM_HEADS == c % MEM_HEADS)
    part = jnp.where(same, s2, 0.0)
    both = part[0:half] + part[half:nr]
    r2 = lax.broadcasted_iota(jnp.int32, (half, nc), 0)
    c2 = lax.broadcasted_iota(jnp.int32, (half, nc), 1)
    cj = (c2 // MEM_HEADS) % 2
    head_ok = c2 % MEM_HEADS == (r2 // CROSS_TD) % MEM_HEADS
    lo = jnp.where(cj == 0, both, 0.0)
    hi = jnp.where(cj == 1, both, 0.0)
    score = both + pltpu.roll(lo, MEM_HEADS, 1) + pltpu.roll(hi, nc - MEM_HEADS, 1)
    sv = jnp.where(head_ok, score, NEG_INF)
    e = jnp.exp(sv - jnp.max(sv, axis=1, keepdims=True))
    p = e * (2.0 / jnp.sum(e, axis=1, keepdims=True))
    pexp = jnp.concatenate([jnp.where(cj == 0, p, 0.0), jnp.where(cj == 1, p, 0.0)],
                           axis=0).astype(BF16)
    return jnp.dot(pexp, vv, preferred_element_type=F32).astype(BF16)


def _cross_sample(qc, mem_k, mem_v, layer):
    bd = mem_k.shape[1]
    td = qc.shape[0] // bd
    assert td == CROSS_TD and MEM_HEAD_DIM == 2 * 128
    rows = 2 * MEM_HEADS * td
    kv_rows = N_MEM * 2 * MEM_HEADS

    def stored_order(x):
        x = x[layer].reshape(bd, N_MEM, MEM_HEADS, 2, 128)
        return jnp.transpose(x, (0, 1, 3, 2, 4)).reshape(bd, kv_rows, 128)

    q2 = jnp.transpose(qc.reshape(bd, td, MEM_HEADS, 2, 128), (0, 3, 2, 1, 4)).reshape(bd, rows, 128)
    kv = pl.BlockSpec((CROSS_BB, kv_rows, 128), lambda b: (b, 0, 0))
    row = pl.BlockSpec((CROSS_BB, rows, 128), lambda b: (b, 0, 0))
    o2 = pl.pallas_call(
        _cross_s_body,
        grid=(bd // CROSS_BB,),
        in_specs=[row, kv, kv],
        out_specs=row,
        out_shape=jax.ShapeDtypeStruct((bd, rows, 128), BF16),
        compiler_params=_params(("parallel",)),
        name="cross_sample",
    )(q2, stored_order(mem_k), stored_order(mem_v))
    o = jnp.transpose(o2.reshape(bd, 2, MEM_HEADS, td, 128), (0, 3, 2, 1, 4))
    return o.reshape(bd * td, D_MODEL)


def _post_body(x1_ref, o_ref, wco_ref, g_ref, wg_ref, wu_ref, wd_ref, fn_ref, y_ref):
    x2 = x1_ref[...] + jnp.dot(o_ref[...], wco_ref[...], preferred_element_type=F32)
    h = _rms(x2, g_ref[...]).astype(BF16)
    x3 = x2 + 0.5 * _swiglu(h, wg_ref, wu_ref, wd_ref)
    y_ref[...] = _rms(x3, fn_ref[...])


def _post(x1, o, wco, g, wg, wu, wd, fn):
    n = x1.shape[0]
    row = pl.BlockSpec((TM, D_MODEL), lambda i: (i, 0))
    return pl.pallas_call(
        _post_body,
        grid=(n // TM,),
        in_specs=[row, row, _const_spec((D_MODEL, D_MODEL)), _const_spec((1, D_MODEL)),
                  _const_spec((D_MODEL, D_FF)), _const_spec((D_MODEL, D_FF)),
                  _const_spec((D_FF, D_MODEL)), _const_spec((1, D_MODEL))],
        out_specs=row,
        out_shape=jax.ShapeDtypeStruct((n, D_MODEL), F32),
        compiler_params=_params(("parallel",)),
        name="cross_out_ffn2",
    )(x1, o, wco, g, wg, wu, wd, fn)


def _rope_tables(pos):
    half = QK_ROPE // 2
    inv = ROPE_BASE ** (-jnp.arange(half, dtype=F32) / half)
    ang = pos.astype(F32)[:, None] * inv[None, :]
    cos = jnp.cos(ang)
    sin = jnp.sin(ang)
    cos_t = jnp.tile(jnp.concatenate([cos, cos], axis=1), (1, N_HEADS))
    sin_t = jnp.tile(jnp.concatenate([-sin, sin], axis=1), (1, N_HEADS))
    return cos_t, sin_t


def _swap_halves(w):
    half = QK_ROPE // 2
    return jnp.concatenate([w[..., half:], w[..., :half]], axis=-1)


def _layer_weights(l, ffn1_norm, ffn1_w_gate, ffn1_w_up, ffn1_w_down, mix_norm, w_in, q_norm,
                   w_q_up, kv_norm, w_k_up, w_v_up, conv_w, conv_b, conv_ln_g, conv_ln_b, w_out,
                   cross_norm, mem_norm, w_cq, w_mk, w_mv, w_co, ffn2_norm, ffn2_w_gate,
                   ffn2_w_up, ffn2_w_down):
    row = lambda v: v[l][None, :]
    wi = w_in[l]
    o1, o2, o3 = Q_LORA, Q_LORA + KV_LORA, Q_LORA + KV_LORA + QK_ROPE
    w_kr = wi[:, o2:o3]
    rep = KR_TILE // QK_ROPE
    win = jnp.concatenate(
        [wi[:, :o1], wi[:, o1:o2], wi[:, o3:o3 + CONV_DIM], wi[:, o3 + CONV_DIM:],
         jnp.tile(w_kr, (1, rep)), jnp.tile(_swap_halves(w_kr), (1, rep))], axis=1).astype(BF16)
    wq3 = w_q_up[l].reshape(Q_LORA, N_HEADS, QK_NOPE + QK_ROPE)
    wq_nope = jnp.pad(wq3[:, :, :QK_NOPE], ((0, 0), (0, 0), (0, NOPE_PAD - QK_NOPE)))
    wq_rope = wq3[:, :, QK_NOPE:]
    wq = jnp.concatenate(
        [wq_nope.reshape(Q_LORA, N_HEADS * NOPE_PAD), wq_rope.reshape(Q_LORA, ROPE_ALL),
         _swap_halves(wq_rope).reshape(Q_LORA, ROPE_ALL)], axis=1).astype(BF16)
    wk = jnp.pad(jnp.transpose(w_k_up[l], (1, 2, 0)),
                 ((0, 0), (0, NOPE_PAD - QK_NOPE), (0, 0))).astype(BF16)
    wv = w_v_up[l]
    eye = jnp.eye(N_HEADS, dtype=wv.dtype)
    wvbd = jnp.einsum('chv,hg->hcgv', wv, eye).reshape(
        N_HEADS * KV_LORA, N_HEADS * V_DIM).astype(BF16)
    w8 = jnp.broadcast_to(conv_w[l][:, None, :], (CONV_WIDTH, 8, CONV_DIM))
    return dict(
        ffn1=(row(ffn1_norm), ffn1_w_gate[l].astype(BF16), ffn1_w_up[l].astype(BF16),
              ffn1_w_down[l].astype(BF16)),
        proj=(row(mix_norm), win, row(q_norm), wq, row(kv_norm), wk),
        conv=(w8, row(conv_b), row(conv_ln_g), row(conv_ln_b)),
        mix=(wvbd, w_out[l].astype(BF16), row(cross_norm), w_cq[l].astype(BF16)),
        mem=(row(mem_norm), w_mk[l].astype(BF16), w_mv[l].astype(BF16)),
        post=(w_co[l].astype(BF16), row(ffn2_norm), ffn2_w_gate[l].astype(BF16),
              ffn2_w_up[l].astype(BF16), ffn2_w_down[l].astype(BF16)),
    )


def kernel(x_prompt, x_sample, mem_prompt, cache_kv_latent, cache_k_rope, state_conv, cache_mem_k, cache_mem_v, page_table, ffn1_norm, ffn1_w_gate, ffn1_w_up, ffn1_w_down, mix_norm, w_in, q_norm, w_q_up, kv_norm, w_k_up, w_v_up, conv_w, conv_b, conv_ln_g, conv_ln_b, w_out, cross_norm, mem_norm, w_cq, w_mk, w_mv, w_co, ffn2_norm, ffn2_w_gate, ffn2_w_up, ffn2_w_down, final_norm):
    batch, seq, _ = x_prompt.shape
    bd, td, _ = x_sample.shape
    depth = ffn1_norm.shape[0]
    assert depth == 1, "the final norm is fused into the last layer's kernel"
    past_len = page_table.shape[1] * PAGE_SIZE
    n_p, n_s = batch * seq, bd * td
    state_w = CONV_WIDTH - 1

    cos_p, sin_p = _rope_tables(jnp.arange(seq, dtype=jnp.int32))
    cos_s, sin_s = _rope_tables(past_len + jnp.arange(td, dtype=jnp.int32))
    cos_s, sin_s = jnp.tile(cos_s, (TM // td, 1)), jnp.tile(sin_s, (TM // td, 1))

    xp = x_prompt.reshape(n_p, D_MODEL)
    xs = x_sample.reshape(n_s, D_MODEL)
    fn = final_norm[None, :]
    outs = {k: [] for k in ("kvl_p", "kr_p", "cs_p", "mk_p", "mv_p", "kvl_s", "kr_s", "cs_s")}
    for l in range(depth):
        w = _layer_weights(l, ffn1_norm, ffn1_w_gate, ffn1_w_up, ffn1_w_down, mix_norm, w_in,
                           q_norm, w_q_up, kv_norm, w_k_up, w_v_up, conv_w, conv_b, conv_ln_g,
                           conv_ln_b, w_out, cross_norm, mem_norm, w_cq, w_mk, w_mv, w_co,
                           ffn2_norm, ffn2_w_gate, ffn2_w_up, ffn2_w_down)
        xp = _ffn(xp, *w["ffn1"])
        xs = _ffn(xs, *w["ffn1"])

        qlat, qrope, kfull, ckv, krope, u = _proj(xp, cos_p, sin_p, seq // TM, *w["proj"])
        olat = _attn_prompt(qlat, qrope, kfull, batch, seq)
        convo = _conv_prompt(u, *w["conv"], batch, seq)
        xp1, qc_p = _mix(xp, olat, convo, *w["mix"])
        outs["kvl_p"].append(ckv.reshape(batch, seq, KV_LORA))
        outs["kr_p"].append(krope.reshape(batch, seq, QK_ROPE))
        outs["cs_p"].append(u.reshape(batch, seq, CONV_DIM)[:, seq - state_w:])

        qlat, qrope, _, ckv, krope, u = _proj(xs, cos_s, sin_s, 1, *w["proj"])
        rows = td * N_HEADS
        olat = _attn_decode(
            page_table,
            qlat.reshape(bd, rows, KV_LORA), qrope.reshape(bd, rows, QK_ROPE),
            ckv.reshape(bd, td, KV_LORA), krope.reshape(bd, td, QK_ROPE),
            cache_kv_latent, jnp.swapaxes(cache_k_rope, 2, 3), l).reshape(n_s, N_HEADS * KV_LORA)
        u3 = u.reshape(bd, td, CONV_DIM)
        ext = jnp.concatenate([state_conv[l], u3], axis=1)
        convo = _conv_sample(jnp.transpose(ext, (1, 0, 2)), *w["conv"])
        convo = jnp.transpose(convo, (1, 0, 2)).reshape(n_s, CONV_DIM)
        xs1, qc_s = _mix(xs, olat, convo, *w["mix"])
        outs["kvl_s"].append(ckv.reshape(bd, td, KV_LORA))
        outs["kr_s"].append(krope.reshape(bd, td, QK_ROPE))
        outs["cs_s"].append(ext[:, td:])

        mk, mv = _memkv(mem_prompt.reshape(batch * N_MEM, D_MODEL), *w["mem"])
        o_p = _cross_prompt(qc_p, mk, mv, batch, seq)
        o_s = _cross_sample(qc_s, cache_mem_k, cache_mem_v, l)
        outs["mk_p"].append(mk.reshape(batch, N_MEM, MEM_HEADS, MEM_HEAD_DIM))
        outs["mv_p"].append(mv.reshape(batch, N_MEM, MEM_HEADS, MEM_HEAD_DIM))

        xp = _post(xp1, o_p, *w["post"], fn)
        xs = _post(xs1, o_s, *w["post"], fn)

    st = lambda k: jnp.stack(outs[k])
    return (xp.reshape(batch, seq, D_MODEL), xs.reshape(bd, td, D_MODEL),
            st("kvl_p"), st("kr_p"), st("cs_p"), st("mk_p"), st("mv_p"),
            st("kvl_s"), st("kr_s"), st("cs_s"))
```

```python
import functools

import jax
import jax.numpy as jnp
from jax import lax
from jax.experimental import pallas as pl
from jax.experimental.pallas import tpu as pltpu

F32 = jnp.float32
BF16 = jnp.bfloat16

D_MODEL = 1024
N_HEADS = 8
QK_NOPE = 64
QK_ROPE = 32
V_DIM = 64
Q_LORA = 384
KV_LORA = 256
CONV_DIM = 512
CONV_WIDTH = 31
D_FF = 2816
N_MEM = 256
MEM_HEADS = 4
MEM_HEAD_DIM = D_MODEL // MEM_HEADS
PAGE_SIZE = 128
ROPE_BASE = 10000.0
EPS = 1e-6
NEG_INF = -1e30
ATTN_SCALE = (QK_NOPE + QK_ROPE) ** -0.5
MEM_SCALE = MEM_HEAD_DIM ** -0.5
QK_LOG2_SCALE = ATTN_SCALE * 1.4426950408889634

VMEM_LIMIT = 56 * 1024 * 1024

TM = 512
NOPE_PAD = 128
ROPE_ALL = N_HEADS * QK_ROPE
KR_TILE = 128
K_FULL = KV_LORA + KR_TILE
O_CQ, O_CKV, O_A, O_GT, O_KR, O_KRS, W_IN_COLS = 0, 384, 640, 1152, 1664, 1792, 1920
O_QN, O_QR, O_QRS, W_Q_COLS = 0, 1024, 1280, 1536

BQ = 256
BK = 512
ATTN_GROUP_HEADS = 2
PAGES_PER_STEP = 64
FFN_DECODE_TM = 128
DECODE_SLOTS = 3
DECODE_SUBCHUNKS = 4
CROSS_TD = 4
CROSS_BB = 4
CONV_TT = 256
CONV_R = 32
CONV_HALO = 32


def _const_spec(shape):
    nd = len(shape)
    return pl.BlockSpec(shape, lambda *_: (0,) * nd, pipeline_mode=pl.Buffered(1))


def _params(sem):
    return pltpu.CompilerParams(dimension_semantics=sem, vmem_limit_bytes=VMEM_LIMIT)


def _rms(x, g):
    ms = jnp.mean(x * x, axis=-1, keepdims=True)
    return x * lax.rsqrt(ms + EPS) * g


def _swiglu(h, wg_ref, wu_ref, wd_ref):
    gate = jnp.dot(h, wg_ref[...], preferred_element_type=F32)
    up = jnp.dot(h, wu_ref[...], preferred_element_type=F32)
    a = (gate * jax.nn.sigmoid(gate) * up).astype(BF16)
    return jnp.dot(a, wd_ref[...], preferred_element_type=F32)


def _ffn_body(x_ref, g_ref, wg_ref, wu_ref, wd_ref, o_ref):
    x = x_ref[...]
    h = _rms(x, g_ref[...]).astype(BF16)
    o_ref[...] = x + 0.5 * _swiglu(h, wg_ref, wu_ref, wd_ref)


def _ffn(x, g, wg, wu, wd):
    n = x.shape[0]
    row = pl.BlockSpec((TM, D_MODEL), lambda i: (i, 0))
    return pl.pallas_call(
        _ffn_body,
        grid=(n // TM,),
        in_specs=[row, _const_spec((1, D_MODEL)), _const_spec((D_MODEL, D_FF)),
                  _const_spec((D_MODEL, D_FF)), _const_spec((D_FF, D_MODEL))],
        out_specs=row,
        out_shape=jax.ShapeDtypeStruct((n, D_MODEL), F32),
        compiler_params=_params(("parallel",)),
        name="ffn1",
    )(x, g, wg, wu, wd)


def _proj_body(x_ref, cos_ref, sin_ref, g_ref, win_ref, qn_ref, wq_ref, kvn_ref, wk_ref,
               qlat_ref, qrope_ref, kfull_ref, ckv_ref, krope_ref, u_ref):
    h = _rms(x_ref[...], g_ref[...]).astype(BF16)
    z = jnp.dot(h, win_ref[...], preferred_element_type=F32)
    cos = cos_ref[...]
    sin = sin_ref[...]
    u_ref[...] = z[:, O_A:O_GT] * jax.nn.sigmoid(z[:, O_GT:O_KR])
    c = _rms(z[:, O_CKV:O_A], kvn_ref[...])
    ckv_ref[...] = c
    kr = z[:, O_KR:O_KRS] * cos[:, :KR_TILE] + z[:, O_KRS:W_IN_COLS] * sin[:, :KR_TILE]
    krope_ref[...] = kr[:, :QK_ROPE]
    kfull_ref[:, 0:KV_LORA] = c.astype(BF16)
    kfull_ref[:, KV_LORA:K_FULL] = kr.astype(BF16)
    cqn = _rms(z[:, O_CQ:O_CKV], qn_ref[...]).astype(BF16)
    q = jnp.dot(cqn, wq_ref[...], preferred_element_type=F32)
    qrope_ref[...] = ((q[:, O_QR:O_QRS] * cos + q[:, O_QRS:W_Q_COLS] * sin)
                      * QK_LOG2_SCALE).astype(BF16)
    for hh in range(N_HEADS):
        qn = q[:, hh * NOPE_PAD:(hh + 1) * NOPE_PAD].astype(BF16)
        qlat_ref[:, hh * KV_LORA:(hh + 1) * KV_LORA] = (jnp.dot(
            qn, wk_ref[hh], preferred_element_type=F32) * QK_LOG2_SCALE).astype(BF16)


def _proj(x, cos, sin, tab_blocks, g, win, qn, wq, kvn, wk):
    n = x.shape[0]
    row = lambda w: pl.BlockSpec((TM, w), lambda i: (i, 0))
    tab = pl.BlockSpec((TM, ROPE_ALL), lambda i: (i % tab_blocks, 0))
    return pl.pallas_call(
        _proj_body,
        grid=(n // TM,),
        in_specs=[row(D_MODEL), tab, tab, _const_spec((1, D_MODEL)),
                  _const_spec((D_MODEL, W_IN_COLS)), _const_spec((1, Q_LORA)),
                  _const_spec((Q_LORA, W_Q_COLS)), _const_spec((1, KV_LORA)),
                  _const_spec((N_HEADS, NOPE_PAD, KV_LORA))],
        out_specs=[row(N_HEADS * KV_LORA), row(ROPE_ALL), row(K_FULL), row(KV_LORA),
                   row(QK_ROPE), row(CONV_DIM)],
        out_shape=[jax.ShapeDtypeStruct((n, N_HEADS * KV_LORA), BF16),
                   jax.ShapeDtypeStruct((n, ROPE_ALL), BF16),
                   jax.ShapeDtypeStruct((n, K_FULL), BF16),
                   jax.ShapeDtypeStruct((n, KV_LORA), F32),
                   jax.ShapeDtypeStruct((n, QK_ROPE), F32),
                   jax.ShapeDtypeStruct((n, CONV_DIM), F32)],
        compiler_params=_params(("parallel",)),
        name="mixer_proj",
    )(x, cos, sin, g, win, qn, wq, kvn, wk)


def _lane_tile(v, n):
    return jnp.concatenate([v] * n, axis=1) if n > 1 else v


def _attn_p_body(qlat_ref, qrope_ref, k_ref, o_ref, qs_ref, m_ref, l_ref, acc_ref):
    i = pl.program_id(1)
    lane = lax.broadcasted_iota(jnp.int32, (BQ, KR_TILE), 1)
    heads_per_tile = KR_TILE // QK_ROPE
    for hh in range(N_HEADS):
        r0 = hh * BQ
        qs_ref[r0:r0 + BQ, 0:KV_LORA] = qlat_ref[:, hh * KV_LORA:(hh + 1) * KV_LORA]
        t = hh // heads_per_tile
        grp = qrope_ref[:, t * KR_TILE:(t + 1) * KR_TILE]
        lo = (hh % heads_per_tile) * QK_ROPE
        keep = jnp.logical_and(lane >= lo, lane < lo + QK_ROPE)
        qs_ref[r0:r0 + BQ, KV_LORA:K_FULL] = jnp.where(keep, grp, jnp.zeros_like(grp))
    m_ref[...] = jnp.full(m_ref.shape, NEG_INF, F32)
    l_ref[...] = jnp.zeros(l_ref.shape, F32)
    acc_ref[...] = jnp.zeros(acc_ref.shape, F32)
    q0 = i * BQ
    n_full = q0 // BK
    n_groups = N_HEADS // ATTN_GROUP_HEADS
    g_rows = ATTN_GROUP_HEADS * BQ

    def part_a(kb, masked):
        start = pl.multiple_of(kb * BK, BK)
        k = k_ref[pl.ds(start, BK), :]
        ps, alphas = [], []
        for g in range(n_groups):
            rs = slice(g * g_rows, (g + 1) * g_rows)
            s = lax.dot_general(qs_ref[rs, :], k, (((1,), (1,)), ((), ())),
                                preferred_element_type=F32)
            if masked:
                r = lax.broadcasted_iota(jnp.int32, (g_rows, BK), 0)
                col = lax.broadcasted_iota(jnp.int32, (g_rows, BK), 1)
                qpos = q0 + jnp.bitwise_and(r, BQ - 1)
                s = jnp.where(start + col <= qpos, s, NEG_INF)
            m_prev = m_ref[rs, :]
            m_next = jnp.maximum(m_prev, jnp.max(s, axis=1, keepdims=True))
            alpha = jnp.exp2(m_prev - m_next)
            p = jnp.exp2(s - _lane_tile(m_next, BK // 128))
            l_ref[rs, :] = alpha * l_ref[rs, :] + jnp.sum(p, axis=1, keepdims=True)
            m_ref[rs, :] = m_next
            ps.append(p.astype(BF16))
            alphas.append(alpha)
        return tuple(ps), tuple(alphas)

    def part_b(kb, ps, alphas):
        start = pl.multiple_of(kb * BK, BK)
        v = k_ref[pl.ds(start, BK), 0:KV_LORA]
        for g in range(n_groups):
            rs = slice(g * g_rows, (g + 1) * g_rows)
            acc_ref[rs, :] = (acc_ref[rs, :] * _lane_tile(alphas[g], KV_LORA // 128)
                              + jnp.dot(ps[g], v, preferred_element_type=F32))

    def trip(kb, carry):
        ps, alphas = part_a(kb, False)
        part_b(jnp.maximum(kb - 1, 0), *carry)
        return ps, alphas

    nothing = (tuple(jnp.zeros((g_rows, BK), BF16) for _ in range(n_groups)),
               tuple(jnp.ones((g_rows, 128), F32) for _ in range(n_groups)))
    lagging = lax.fori_loop(0, n_full, trip, nothing)
    ps, alphas = part_a(n_full, True)
    part_b(jnp.maximum(n_full - 1, 0), *lagging)
    part_b(n_full, ps, alphas)
    inv = 1.0 / l_ref[...]
    for hh in range(N_HEADS):
        r0 = hh * BQ
        o_ref[:, hh * KV_LORA:(hh + 1) * KV_LORA] = (
            acc_ref[r0:r0 + BQ, :] * _lane_tile(inv[r0:r0 + BQ, :], KV_LORA // 128)).astype(BF16)


def _attn_prompt(qlat, qrope, kfull, batch, seq):
    nq = seq // BQ
    rows = N_HEADS * BQ
    return pl.pallas_call(
        _attn_p_body,
        grid=(batch, nq),
        in_specs=[pl.BlockSpec((BQ, N_HEADS * KV_LORA), lambda b, i: (b * nq + i, 0)),
                  pl.BlockSpec((BQ, ROPE_ALL), lambda b, i: (b * nq + i, 0)),
                  pl.BlockSpec((seq, K_FULL), lambda b, i: (b, 0))],
        out_specs=pl.BlockSpec((BQ, N_HEADS * KV_LORA), lambda b, i: (b * nq + i, 0)),
        out_shape=jax.ShapeDtypeStruct((batch * seq, N_HEADS * KV_LORA), BF16),
        scratch_shapes=[pltpu.VMEM((rows, K_FULL), BF16), pltpu.VMEM((rows, 128), F32),
                        pltpu.VMEM((rows, 128), F32), pltpu.VMEM((rows, KV_LORA), F32)],
        compiler_params=_params(("parallel", "arbitrary")),
        name="attn_prompt",
    )(qlat, qrope, kfull)


def _decode_chunk(t, first, last, n_chunks, pt_ref, q_ref, qr_ref, cnew_ref, krnew_ref,
                  cc_hbm, ckr_hbm, o_ref, cbuf, krb, m_ref, l_ref, acc_ref, sem_c, sem_k, layer):
    g_pages = cbuf.shape[1]
    ahead = DECODE_SLOTS - 1
    slot = lax.rem(t, DECODE_SLOTS)
    rows = q_ref.shape[0]
    td = rows // N_HEADS

    def page_copies(chunk, sl, g):
        page = pt_ref[chunk * g_pages + g]
        return (pltpu.make_async_copy(cc_hbm.at[layer, page], cbuf.at[sl, g], sem_c.at[sl]),
                pltpu.make_async_copy(ckr_hbm.at[layer, page], krb.at[sl, g], sem_k.at[sl]))

    def start_chunk(chunk):
        sl = lax.rem(chunk, DECODE_SLOTS)

        def issue(g, carry):
            for cp in page_copies(chunk, sl, g):
                cp.start()
            return carry
        lax.fori_loop(0, g_pages, issue, 0, unroll=8)

    @pl.when(t == 0)
    def _():
        for c in range(ahead):
            start_chunk(c)

    @pl.when(t + ahead < n_chunks)
    def _():
        start_chunk(t + ahead)

    def wait_page(g, carry):
        for cp in page_copies(t, slot, g):
            cp.wait()
        return carry
    lax.fori_loop(0, g_pages, wait_page, 0, unroll=8)

    if first:
        m_ref[...] = jnp.full(m_ref.shape, NEG_INF, F32)
        l_ref[...] = jnp.zeros(l_ref.shape, F32)
        acc_ref[...] = jnp.zeros(acc_ref.shape, F32)

    q = q_ref[...].astype(F32)
    qr = qr_ref[...].astype(F32)
    sub_pages = g_pages // DECODE_SUBCHUNKS
    sub = sub_pages * PAGE_SIZE
    keys, scores = [], []
    for h in range(DECODE_SUBCHUNKS):
        p0 = h * sub_pages
        ks = cbuf[slot, p0:p0 + sub_pages].reshape(sub, KV_LORA)
        krs = jnp.concatenate([krb[slot, p0 + g] for g in range(sub_pages)], axis=1)
        keys.append(ks)
        scores.append(lax.dot_general(q, ks, (((1,), (1,)), ((), ())), preferred_element_type=F32)
                      + jnp.dot(qr, krs, preferred_element_type=F32))
    m_next = m_ref[...]
    l_new = l_ref[...]
    acc = acc_ref[...]
    for ks, s in zip(keys, scores):
        m_prev = m_next
        m_next = jnp.maximum(m_prev, jnp.max(s, axis=1, keepdims=True))
        alpha = jnp.exp2(m_prev - m_next)
        p = jnp.exp2(s - _lane_tile(m_next, sub // 128))
        l_new = alpha * l_new + jnp.sum(p, axis=1, keepdims=True)
        acc = (acc * _lane_tile(alpha, KV_LORA // 128)
               + jnp.dot(p.astype(BF16).astype(F32), ks, preferred_element_type=F32))
    if not last:
        l_ref[...] = l_new
        m_ref[...] = m_next
        acc_ref[...] = acc
        return

    tok = lax.broadcasted_iota(jnp.int32, (rows, 1), 0) // N_HEADS
    s_new = []
    for t2 in range(td):
        cn = cnew_ref[t2:t2 + 1, :].astype(BF16).astype(F32)
        kn = krnew_ref[t2:t2 + 1, :].astype(BF16).astype(F32)
        st = jnp.sum(q * cn, axis=1, keepdims=True) + jnp.sum(qr * kn, axis=1, keepdims=True)
        s_new.append(jnp.where(t2 <= tok, st, NEG_INF))
    m_cur = s_new[0]
    for st in s_new[1:]:
        m_cur = jnp.maximum(m_cur, st)
    m_fin = jnp.maximum(m_next, m_cur)
    alpha = jnp.exp2(m_next - m_fin)
    l_fin = alpha * l_new
    acc_fin = acc * _lane_tile(alpha, KV_LORA // 128)
    for t2 in range(td):
        pt = jnp.exp2(s_new[t2] - m_fin)
        l_fin = l_fin + pt
        cn = cnew_ref[t2:t2 + 1, :].astype(BF16).astype(F32)
        acc_fin = acc_fin + _lane_tile(pt.astype(BF16).astype(F32), KV_LORA // 128) * cn
    o_ref[...] = (acc_fin * _lane_tile(1.0 / l_fin, KV_LORA // 128)).astype(BF16)


def _ffn_decode_body(pt_ref, x_ref, g_ref, wg_ref, wu_ref, wd_ref, qlat_ref, qr_ref, cnew_ref,
                     krnew_ref, cc_hbm, ckr_hbm, y_ref, o_ref, cbuf, krb, m_ref, l_ref, acc_ref,
                     sem_c, sem_k, *, layer, chunks_per_req):
    reqs = qlat_ref.shape[0]
    chunks = reqs * chunks_per_req
    i = pl.program_id(0)
    n_chunks = pl.num_programs(0) * chunks
    x = x_ref[...]
    h = _rms(x, g_ref[...]).astype(BF16)
    y_ref[...] = x + 0.5 * _swiglu(h, wg_ref, wu_ref, wd_ref)
    for c in range(chunks):
        r, j = divmod(c, chunks_per_req)
        _decode_chunk(i * chunks + c, j == 0, j == chunks_per_req - 1, n_chunks, pt_ref,
                      qlat_ref.at[r], qr_ref.at[r], cnew_ref.at[r], krnew_ref.at[r], cc_hbm,
                      ckr_hbm, o_ref.at[r], cbuf, krb, m_ref, l_ref, acc_ref, sem_c, sem_k, layer)


def _ffn_decode(x, g, wg, wu, wd, page_table, qlat, qrope, cnew, krnew, cache_c, cache_kr_t, layer):
    n = x.shape[0]
    bd, rows, _ = qlat.shape
    td = rows // N_HEADS
    n_pages = page_table.shape[1]
    g_pages = PAGES_PER_STEP
    chunks_per_req = n_pages // g_pages
    n_steps = n // FFN_DECODE_TM
    reqs = bd // n_steps
    assert reqs * n_steps == bd and n_steps * FFN_DECODE_TM == n
    row = pl.BlockSpec((FFN_DECODE_TM, D_MODEL), lambda i, pt: (i, 0))
    per_req = lambda r, w: pl.BlockSpec((reqs, r, w), lambda i, pt: (i, 0, 0))
    const = lambda shape: pl.BlockSpec(shape, lambda i, pt: (0,) * len(shape),
                                       pipeline_mode=pl.Buffered(1))
    hbm = pl.BlockSpec(memory_space=pl.ANY)
    grid_spec = pltpu.PrefetchScalarGridSpec(
        num_scalar_prefetch=1,
        grid=(n_steps,),
        in_specs=[row, const((1, D_MODEL)), const((D_MODEL, D_FF)), const((D_MODEL, D_FF)),
                  const((D_FF, D_MODEL)), per_req(rows, KV_LORA), per_req(rows, QK_ROPE),
                  per_req(td, KV_LORA), per_req(td, QK_ROPE), hbm, hbm],
        out_specs=[row, per_req(rows, KV_LORA)],
        scratch_shapes=[pltpu.VMEM((DECODE_SLOTS, g_pages, PAGE_SIZE, KV_LORA), F32),
                        pltpu.VMEM((DECODE_SLOTS, g_pages, QK_ROPE, PAGE_SIZE), F32),
                        pltpu.VMEM((rows, 128), F32), pltpu.VMEM((rows, 128), F32),
                        pltpu.VMEM((rows, KV_LORA), F32),
                        pltpu.SemaphoreType.DMA((DECODE_SLOTS,)),
                        pltpu.SemaphoreType.DMA((DECODE_SLOTS,))])
    return pl.pallas_call(
        functools.partial(_ffn_decode_body, layer=layer, chunks_per_req=chunks_per_req),
        grid_spec=grid_spec,
        out_shape=[jax.ShapeDtypeStruct((n, D_MODEL), F32),
                   jax.ShapeDtypeStruct((bd, rows, KV_LORA), BF16)],
        compiler_params=_params(("arbitrary",)),
        name="ffn1_attn_decode",
    )(page_table.reshape(-1), x, g, wg, wu, wd, qlat, qrope, cnew, krnew, cache_c, cache_kr_t)


def _ln_silu(y, g, b):
    mu = jnp.mean(y, axis=-1, keepdims=True)
    d = y - mu
    var = jnp.mean(d * d, axis=-1, keepdims=True)
    z = d * lax.rsqrt(var + EPS) * g + b
    return z * jax.nn.sigmoid(z)


def _conv_p_body(prev_ref, cur_ref, w_ref, b_ref, g_ref, beta_ref, o_ref, xb_ref):
    i = pl.program_id(1)
    xb_ref[0:CONV_HALO, :] = jnp.where(i > 0, prev_ref[...], 0.0)
    xb_ref[CONV_HALO:CONV_HALO + CONV_TT, :] = cur_ref[...]
    off = CONV_HALO - (CONV_WIDTH - 1)
    for r in range(CONV_TT // CONV_R):
        acc = jnp.zeros((CONV_R // 8, 8, CONV_DIM), F32)
        for k in range(CONV_WIDTH):
            lo = r * CONV_R + off + k
            xs = xb_ref[lo:lo + CONV_R, :].reshape(CONV_R // 8, 8, CONV_DIM)
            acc = acc + xs * w_ref[k][None]
        y = acc.reshape(CONV_R, CONV_DIM) + b_ref[...]
        o_ref[r * CONV_R:(r + 1) * CONV_R, :] = _ln_silu(y, g_ref[...], beta_ref[...]).astype(BF16)


def _conv_prompt(u, w8, b, g, beta, batch, seq):
    nt = seq // CONV_TT
    halo_per_tile = CONV_TT // CONV_HALO
    halo_per_seq = seq // CONV_HALO
    return pl.pallas_call(
        _conv_p_body,
        grid=(batch, nt),
        in_specs=[pl.BlockSpec((CONV_HALO, CONV_DIM),
                               lambda bb, i: (bb * halo_per_seq + jnp.maximum(i * halo_per_tile - 1, 0), 0)),
                  pl.BlockSpec((CONV_TT, CONV_DIM), lambda bb, i: (bb * nt + i, 0)),
                  _const_spec((CONV_WIDTH, 8, CONV_DIM)), _const_spec((1, CONV_DIM)),
                  _const_spec((1, CONV_DIM)), _const_spec((1, CONV_DIM))],
        out_specs=pl.BlockSpec((CONV_TT, CONV_DIM), lambda bb, i: (bb * nt + i, 0)),
        out_shape=jax.ShapeDtypeStruct((batch * seq, CONV_DIM), BF16),
        scratch_shapes=[pltpu.VMEM((CONV_HALO + CONV_TT, CONV_DIM), F32)],
        compiler_params=_params(("parallel", "arbitrary")),
        name="conv_prompt",
    )(u, u, w8, b, g, beta)


def _conv_s_body(ext_ref, w_ref, b_ref, g_ref, beta_ref, o_ref):
    td, bd, _ = o_ref.shape
    for rg in range(bd // CONV_R):
        for t in range(td):
            acc = jnp.zeros((CONV_R // 8, 8, CONV_DIM), F32)
            for k in range(CONV_WIDTH):
                xs = ext_ref[t + k, rg * CONV_R:(rg + 1) * CONV_R, :].reshape(CONV_R // 8, 8, CONV_DIM)
                acc = acc + xs * w_ref[k][None]
            y = acc.reshape(CONV_R, CONV_DIM) + b_ref[...]
            o_ref[t, rg * CONV_R:(rg + 1) * CONV_R, :] = _ln_silu(
                y, g_ref[...], beta_ref[...]).astype(BF16)


def _conv_sample(ext_t, w8, b, g, beta):
    n_ext, bd, _ = ext_t.shape
    td = n_ext - (CONV_WIDTH - 1)
    return pl.pallas_call(
        _conv_s_body,
        out_shape=jax.ShapeDtypeStruct((td, bd, CONV_DIM), BF16),
        compiler_params=pltpu.CompilerParams(vmem_limit_bytes=VMEM_LIMIT),
        name="conv_sample",
    )(ext_t, w8, b, g, beta)


def _mix_body(x_ref, ol_ref, cv_ref, wv_ref, wout_ref, cg_ref, wcq_ref, x1_ref, qc_ref):
    half = N_HEADS * V_DIM
    mla = jnp.dot(ol_ref[...], wv_ref[...], preferred_element_type=F32).astype(BF16)
    x1 = (x_ref[...]
          + jnp.dot(mla, wout_ref[0:half, :], preferred_element_type=F32)
          + jnp.dot(cv_ref[...], wout_ref[half:, :], preferred_element_type=F32))
    x1_ref[...] = x1
    hq = _rms(x1, cg_ref[...]).astype(BF16)
    qc_ref[...] = (jnp.dot(hq, wcq_ref[...], preferred_element_type=F32) * MEM_SCALE).astype(BF16)


def _mix(x, olat, convo, wvbd, wout, cg, wcq):
    n = x.shape[0]
    row = lambda w: pl.BlockSpec((TM, w), lambda i: (i, 0))
    return pl.pallas_call(
        _mix_body,
        grid=(n // TM,),
        in_specs=[row(D_MODEL), row(N_HEADS * KV_LORA), row(CONV_DIM),
                  _const_spec((N_HEADS * KV_LORA, N_HEADS * V_DIM)),
                  _const_spec((D_MODEL, D_MODEL)), _const_spec((1, D_MODEL)),
                  _const_spec((D_MODEL, D_MODEL))],
        out_specs=[row(D_MODEL), row(D_MODEL)],
        out_shape=[jax.ShapeDtypeStruct((n, D_MODEL), F32),
                   jax.ShapeDtypeStruct((n, D_MODEL), BF16)],
        compiler_params=_params(("parallel",)),
        name="mix_out",
    )(x, olat, convo, wvbd, wout, cg, wcq)


def _memkv_body(mem_ref, g_ref, wk_ref, wv_ref, k_ref, v_ref):
    mn = _rms(mem_ref[...], g_ref[...]).astype(BF16)
    k_ref[...] = jnp.dot(mn, wk_ref[...], preferred_element_type=F32)
    v_ref[...] = jnp.dot(mn, wv_ref[...], preferred_element_type=F32)


def _memkv(mem, g, wk, wv):
    n = mem.shape[0]
    row = pl.BlockSpec((N_MEM, D_MODEL), lambda i: (i, 0))
    return pl.pallas_call(
        _memkv_body,
        grid=(n // N_MEM,),
        in_specs=[row, _const_spec((1, D_MODEL)), _const_spec((D_MODEL, D_MODEL)),
                  _const_spec((D_MODEL, D_MODEL))],
        out_specs=[row, row],
        out_shape=[jax.ShapeDtypeStruct((n, D_MODEL), F32)] * 2,
        compiler_params=_params(("parallel",)),
        name="memory_kv",
    )(mem, g, wk, wv)


def _cross_head(q, k, v):
    s = lax.dot_general(q, k.astype(BF16), (((1,), (1,)), ((), ())), preferred_element_type=F32)
    e = jnp.exp(s - jnp.max(s, axis=1, keepdims=True))
    p = (e * (1.0 / jnp.sum(e, axis=1, keepdims=True))).astype(BF16)
    return jnp.dot(p, v.astype(BF16), preferred_element_type=F32).astype(BF16)


def _cross_p_body(q_ref, k_ref, v_ref, o_ref):
    for hh in range(MEM_HEADS):
        sl = slice(hh * MEM_HEAD_DIM, (hh + 1) * MEM_HEAD_DIM)
        o_ref[:, sl] = _cross_head(q_ref[:, sl], k_ref[:, sl], v_ref[:, sl])


def _cross_prompt(qc, mk, mv, batch, seq):
    nt = seq // TM
    kv = pl.BlockSpec((N_MEM, D_MODEL), lambda b, i: (b, 0))
    row = pl.BlockSpec((TM, D_MODEL), lambda b, i: (b * nt + i, 0))
    return pl.pallas_call(
        _cross_p_body,
        grid=(batch, nt),
        in_specs=[row, kv, kv],
        out_specs=row,
        out_shape=jax.ShapeDtypeStruct((batch * seq, D_MODEL), BF16),
        compiler_params=_params(("parallel", "arbitrary")),
        name="cross_prompt",
    )(qc, mk, mv)


def _cross_s_body(q_ref, k_ref, v_ref, o_ref):
    for i in range(q_ref.shape[0]):
        o_ref[i] = _cross_s_one(q_ref[i], k_ref[i], v_ref[i])


def _cross_s_one(q, k, v):
    kv = k.astype(BF16)
    vv = v.astype(BF16)
    s2 = lax.dot_general(q, kv, (((1,), (1,)), ((), ())), preferred_element_type=F32)
    nr, nc = s2.shape
    half = nr // 2
    r = lax.broadcasted_iota(jnp.int32, (nr, nc), 0)
    c = lax.broadcasted_iota(jnp.int32, (nr, nc), 1)
    same = jnp.logical_and(r // half == (c // MEM_HEADS) % 2,
                           (r // CROSS_TD) % MEM_HEADS == c % MEM_HEADS)
    part = jnp.where(same, s2, 0.0)
    both = part[0:half] + part[half:nr]
    r2 = lax.broadcasted_iota(jnp.int32, (half, nc), 0)
    c2 = lax.broadcasted_iota(jnp.int32, (half, nc), 1)
    cj = (c2 // MEM_HEADS) % 2
    head_ok = c2 % MEM_HEADS == (r2 // CROSS_TD) % MEM_HEADS
    lo = jnp.where(cj == 0, both, 0.0)
    hi = jnp.where(cj == 1, both, 0.0)
    score = both + pltpu.roll(lo, MEM_HEADS, 1) + pltpu.roll(hi, nc - MEM_HEADS, 1)
    sv = jnp.where(head_ok, score, NEG_INF)
    e = jnp.exp(sv - jnp.max(sv, axis=1, keepdims=True))
    p = e * (2.0 / jnp.sum(e, axis=1, keepdims=True))
    pexp = jnp.concatenate([jnp.where(cj == 0, p, 0.0), jnp.where(cj == 1, p, 0.0)],
                           axis=0).astype(BF16)
    return jnp.dot(pexp, vv, preferred_element_type=F32).astype(BF16)


def _cross_sample(qc, mem_k, mem_v, layer):
    bd = mem_k.shape[1]
    td = qc.shape[0] // bd
    assert td == CROSS_TD and MEM_HEAD_DIM == 2 * 128
    rows = 2 * MEM_HEADS * td
    kv_rows = N_MEM * 2 * MEM_HEADS

    def stored_order(x):
        x = x[layer].reshape(bd, N_MEM, MEM_HEADS, 2, 128)
        return jnp.transpose(x, (0, 1, 3, 2, 4)).reshape(bd, kv_rows, 128)

    q2 = jnp.transpose(qc.reshape(bd, td, MEM_HEADS, 2, 128), (0, 3, 2, 1, 4)).reshape(bd, rows, 128)
    kv = pl.BlockSpec((CROSS_BB, kv_rows, 128), lambda b: (b, 0, 0))
    row = pl.BlockSpec((CROSS_BB, rows, 128), lambda b: (b, 0, 0))
    o2 = pl.pallas_call(
        _cross_s_body,
        grid=(bd // CROSS_BB,),
        in_specs=[row, kv, kv],
        out_specs=row,
        out_shape=jax.ShapeDtypeStruct((bd, rows, 128), BF16),
        compiler_params=_params(("parallel",)),
        name="cross_sample",
    )(q2, stored_order(mem_k), stored_order(mem_v))
    o = jnp.transpose(o2.reshape(bd, 2, MEM_HEADS, td, 128), (0, 3, 2, 1, 4))
    return o.reshape(bd * td, D_MODEL)


def _post_body(x1_ref, o_ref, wco_ref, g_ref, wg_ref, wu_ref, wd_ref, fn_ref, y_ref):
    x2 = x1_ref[...] + jnp.dot(o_ref[...], wco_ref[...], preferred_element_type=F32)
    h = _rms(x2, g_ref[...]).astype(BF16)
    x3 = x2 + 0.5 * _swiglu(h, wg_ref, wu_ref, wd_ref)
    y_ref[...] = _rms(x3, fn_ref[...])


def _post(x1, o, wco, g, wg, wu, wd, fn):
    n = x1.shape[0]
    row = pl.BlockSpec((TM, D_MODEL), lambda i: (i, 0))
    return pl.pallas_call(
        _post_body,
        grid=(n // TM,),
        in_specs=[row, row, _const_spec((D_MODEL, D_MODEL)), _const_spec((1, D_MODEL)),
                  _const_spec((D_MODEL, D_FF)), _const_spec((D_MODEL, D_FF)),
                  _const_spec((D_FF, D_MODEL)), _const_spec((1, D_MODEL))],
        out_specs=row,
        out_shape=jax.ShapeDtypeStruct((n, D_MODEL), F32),
        compiler_params=_params(("parallel",)),
        name="cross_out_ffn2",
    )(x1, o, wco, g, wg, wu, wd, fn)


def _rope_tables(pos):
    half = QK_ROPE // 2
    inv = ROPE_BASE ** (-jnp.arange(half, dtype=F32) / half)
    ang = pos.astype(F32)[:, None] * inv[None, :]
    cos = jnp.cos(ang)
    sin = jnp.sin(ang)
    cos_t = jnp.tile(jnp.concatenate([cos, cos], axis=1), (1, N_HEADS))
    sin_t = jnp.tile(jnp.concatenate([-sin, sin], axis=1), (1, N_HEADS))
    return cos_t, sin_t


def _swap_halves(w):
    half = QK_ROPE // 2
    return jnp.concatenate([w[..., half:], w[..., :half]], axis=-1)


def _layer_weights(l, ffn1_norm, ffn1_w_gate, ffn1_w_up, ffn1_w_down, mix_norm, w_in, q_norm,
                   w_q_up, kv_norm, w_k_up, w_v_up, conv_w, conv_b, conv_ln_g, conv_ln_b, w_out,
                   cross_norm, mem_norm, w_cq, w_mk, w_mv, w_co, ffn2_norm, ffn2_w_gate,
                   ffn2_w_up, ffn2_w_down):
    row = lambda v: v[l][None, :]
    wi = w_in[l]
    o1, o2, o3 = Q_LORA, Q_LORA + KV_LORA, Q_LORA + KV_LORA + QK_ROPE
    w_kr = wi[:, o2:o3]
    rep = KR_TILE // QK_ROPE
    win = jnp.concatenate(
        [wi[:, :o1], wi[:, o1:o2], wi[:, o3:o3 + CONV_DIM], wi[:, o3 + CONV_DIM:],
         jnp.tile(w_kr, (1, rep)), jnp.tile(_swap_halves(w_kr), (1, rep))], axis=1).astype(BF16)
    wq3 = w_q_up[l].reshape(Q_LORA, N_HEADS, QK_NOPE + QK_ROPE)
    wq_nope = jnp.pad(wq3[:, :, :QK_NOPE], ((0, 0), (0, 0), (0, NOPE_PAD - QK_NOPE)))
    wq_rope = wq3[:, :, QK_NOPE:]
    wq = jnp.concatenate(
        [wq_nope.reshape(Q_LORA, N_HEADS * NOPE_PAD), wq_rope.reshape(Q_LORA, ROPE_ALL),
         _swap_halves(wq_rope).reshape(Q_LORA, ROPE_ALL)], axis=1).astype(BF16)
    wk = jnp.pad(jnp.transpose(w_k_up[l], (1, 2, 0)),
                 ((0, 0), (0, NOPE_PAD - QK_NOPE), (0, 0))).astype(BF16)
    wv = w_v_up[l]
    eye = jnp.eye(N_HEADS, dtype=wv.dtype)
    wvbd = jnp.einsum('chv,hg->hcgv', wv, eye).reshape(
        N_HEADS * KV_LORA, N_HEADS * V_DIM).astype(BF16)
    w8 = jnp.broadcast_to(conv_w[l][:, None, :], (CONV_WIDTH, 8, CONV_DIM))
    return dict(
        ffn1=(row(ffn1_norm), ffn1_w_gate[l].astype(BF16), ffn1_w_up[l].astype(BF16),
              ffn1_w_down[l].astype(BF16)),
        proj=(row(mix_norm), win, row(q_norm), wq, row(kv_norm), wk),
        conv=(w8, row(conv_b), row(conv_ln_g), row(conv_ln_b)),
        mix=(wvbd, w_out[l].astype(BF16), row(cross_norm), w_cq[l].astype(BF16)),
        mem=(row(mem_norm), w_mk[l].astype(BF16), w_mv[l].astype(BF16)),
        post=(w_co[l].astype(BF16), row(ffn2_norm), ffn2_w_gate[l].astype(BF16),
              ffn2_w_up[l].astype(BF16), ffn2_w_down[l].astype(BF16)),
    )


def kernel(x_prompt, x_sample, mem_prompt, cache_kv_latent, cache_k_rope, state_conv, cache_mem_k, cache_mem_v, page_table, ffn1_norm, ffn1_w_gate, ffn1_w_up, ffn1_w_down, mix_norm, w_in, q_norm, w_q_up, kv_norm, w_k_up, w_v_up, conv_w, conv_b, conv_ln_g, conv_ln_b, w_out, cross_norm, mem_norm, w_cq, w_mk, w_mv, w_co, ffn2_norm, ffn2_w_gate, ffn2_w_up, ffn2_w_down, final_norm):
    batch, seq, _ = x_prompt.shape
    bd, td, _ = x_sample.shape
    depth = ffn1_norm.shape[0]
    assert depth == 1, "the final norm is fused into the last layer's kernel"
    past_len = page_table.shape[1] * PAGE_SIZE
    n_p, n_s = batch * seq, bd * td
    state_w = CONV_WIDTH - 1

    cos_p, sin_p = _rope_tables(jnp.arange(seq, dtype=jnp.int32))
    cos_s, sin_s = _rope_tables(past_len + jnp.arange(td, dtype=jnp.int32))
    cos_s, sin_s = jnp.tile(cos_s, (TM // td, 1)), jnp.tile(sin_s, (TM // td, 1))

    xp = x_prompt.reshape(n_p, D_MODEL)
    xs = x_sample.reshape(n_s, D_MODEL)
    fn = final_norm[None, :]
    outs = {k: [] for k in ("kvl_p", "kr_p", "cs_p", "mk_p", "mv_p", "kvl_s", "kr_s", "cs_s")}
    for l in range(depth):
        w = _layer_weights(l, ffn1_norm, ffn1_w_gate, ffn1_w_up, ffn1_w_down, mix_norm, w_in,
                           q_norm, w_q_up, kv_norm, w_k_up, w_v_up, conv_w, conv_b, conv_ln_g,
                           conv_ln_b, w_out, cross_norm, mem_norm, w_cq, w_mk, w_mv, w_co,
                           ffn2_norm, ffn2_w_gate, ffn2_w_up, ffn2_w_down)
        xs = _ffn(xs, *w["ffn1"])
        qlat_s, qrope_s, _, ckv_s, krope_s, u_s = _proj(xs, cos_s, sin_s, 1, *w["proj"])
        rows = td * N_HEADS
        xp, olat_s = _ffn_decode(
            xp, *w["ffn1"], page_table,
            qlat_s.reshape(bd, rows, KV_LORA), qrope_s.reshape(bd, rows, QK_ROPE),
            ckv_s.reshape(bd, td, KV_LORA), krope_s.reshape(bd, td, QK_ROPE),
            cache_kv_latent, jnp.swapaxes(cache_k_rope, 2, 3), l)

        qlat, qrope, kfull, ckv, krope, u = _proj(xp, cos_p, sin_p, seq // TM, *w["proj"])
        olat = _attn_prompt(qlat, qrope, kfull, batch, seq)
        convo = _conv_prompt(u, *w["conv"], batch, seq)
        xp1, qc_p = _mix(xp, olat, convo, *w["mix"])
        outs["kvl_p"].append(ckv.reshape(batch, seq, KV_LORA))
        outs["kr_p"].append(krope.reshape(batch, seq, QK_ROPE))
        outs["cs_p"].append(u.reshape(batch, seq, CONV_DIM)[:, seq - state_w:])

        ckv, krope, u = ckv_s, krope_s, u_s
        olat = olat_s.reshape(n_s, N_HEADS * KV_LORA)
        u3 = u.reshape(bd, td, CONV_DIM)
        ext = jnp.concatenate([state_conv[l], u3], axis=1)
        convo = _conv_sample(jnp.transpose(ext, (1, 0, 2)), *w["conv"])
        convo = jnp.transpose(convo, (1, 0, 2)).reshape(n_s, CONV_DIM)
        xs1, qc_s = _mix(xs, olat, convo, *w["mix"])
        outs["kvl_s"].append(ckv.reshape(bd, td, KV_LORA))
        outs["kr_s"].append(krope.reshape(bd, td, QK_ROPE))
        outs["cs_s"].append(ext[:, td:])

        mk, mv = _memkv(mem_prompt.reshape(batch * N_MEM, D_MODEL), *w["mem"])
        o_p = _cross_prompt(qc_p, mk, mv, batch, seq)
        o_s = _cross_sample(qc_s, cache_mem_k, cache_mem_v, l)
        outs["mk_p"].append(mk.reshape(batch, N_MEM, MEM_HEADS, MEM_HEAD_DIM))
        outs["mv_p"].append(mv.reshape(batch, N_MEM, MEM_HEADS, MEM_HEAD_DIM))

        xp = _post(xp1, o_p, *w["post"], fn)
        xs = _post(xs1, o_s, *w["post"], fn)

    st = lambda k: jnp.stack(outs[k])
    return (xp.reshape(batch, seq, D_MODEL), xs.reshape(bd, td, D_MODEL),
            st("kvl_p"), st("kr_p"), st("cs_p"), st("mk_p"), st("mv_p"),
            st("kvl_s"), st("kr_s"), st("cs_s"))
```

```python
import functools

import jax
import jax.numpy as jnp
from jax import lax
from jax.experimental import pallas as pl
from jax.experimental.pallas import tpu as pltpu

F32 = jnp.float32
BF16 = jnp.bfloat16

D_MODEL = 1024
N_HEADS = 8
QK_NOPE = 64
QK_ROPE = 32
V_DIM = 64
Q_LORA = 384
KV_LORA = 256
CONV_DIM = 512
CONV_WIDTH = 31
D_FF = 2816
N_MEM = 256
MEM_HEADS = 4
MEM_HEAD_DIM = D_MODEL // MEM_HEADS
PAGE_SIZE = 128
ROPE_BASE = 10000.0
EPS = 1e-6
NEG_INF = -1e30
ATTN_SCALE = (QK_NOPE + QK_ROPE) ** -0.5
MEM_SCALE = MEM_HEAD_DIM ** -0.5
QK_LOG2_SCALE = ATTN_SCALE * 1.4426950408889634

VMEM_LIMIT = 56 * 1024 * 1024

TM = 512
NOPE_PAD = 128
ROPE_ALL = N_HEADS * QK_ROPE
KR_TILE = 128
K_FULL = KV_LORA + KR_TILE
O_CQ, O_CKV, O_A, O_GT, O_KR, O_KRS, W_IN_COLS = 0, 384, 640, 1152, 1664, 1792, 1920
O_QN, O_QR, O_QRS, W_Q_COLS = 0, 1024, 1280, 1536

BQ = 512
BK = 512
ATTN_GROUP_HEADS = 1
PAGES_PER_STEP = 64
FFN_DECODE_TM = 128
DECODE_SLOTS = 3
DECODE_SUBCHUNKS = 4
CROSS_TD = 4
CROSS_BB = 4
CONV_TT = 256
CONV_R = 32
CONV_HALO = 32


def _const_spec(shape):
    nd = len(shape)
    return pl.BlockSpec(shape, lambda *_: (0,) * nd, pipeline_mode=pl.Buffered(1))


def _params(sem):
    return pltpu.CompilerParams(dimension_semantics=sem, vmem_limit_bytes=VMEM_LIMIT)


def _rms(x, g):
    ms = jnp.mean(x * x, axis=-1, keepdims=True)
    return x * lax.rsqrt(ms + EPS) * g


def _swiglu(h, wg_ref, wu_ref, wd_ref):
    gate = jnp.dot(h, wg_ref[...], preferred_element_type=F32)
    up = jnp.dot(h, wu_ref[...], preferred_element_type=F32)
    a = (gate * jax.nn.sigmoid(gate) * up).astype(BF16)
    return jnp.dot(a, wd_ref[...], preferred_element_type=F32)


def _ffn_body(x_ref, g_ref, wg_ref, wu_ref, wd_ref, o_ref):
    x = x_ref[...]
    h = _rms(x, g_ref[...]).astype(BF16)
    o_ref[...] = x + 0.5 * _swiglu(h, wg_ref, wu_ref, wd_ref)


def _ffn(x, g, wg, wu, wd):
    n = x.shape[0]
    row = pl.BlockSpec((TM, D_MODEL), lambda i: (i, 0))
    return pl.pallas_call(
        _ffn_body,
        grid=(n // TM,),
        in_specs=[row, _const_spec((1, D_MODEL)), _const_spec((D_MODEL, D_FF)),
                  _const_spec((D_MODEL, D_FF)), _const_spec((D_FF, D_MODEL))],
        out_specs=row,
        out_shape=jax.ShapeDtypeStruct((n, D_MODEL), F32),
        compiler_params=_params(("parallel",)),
        name="ffn1",
    )(x, g, wg, wu, wd)


def _proj_body(x_ref, cos_ref, sin_ref, g_ref, win_ref, qn_ref, wq_ref, kvn_ref, wk_ref,
               qlat_ref, qrope_ref, kfull_ref, ckv_ref, krope_ref, u_ref):
    h = _rms(x_ref[...], g_ref[...]).astype(BF16)
    z = jnp.dot(h, win_ref[...], preferred_element_type=F32)
    cos = cos_ref[...]
    sin = sin_ref[...]
    u_ref[...] = z[:, O_A:O_GT] * jax.nn.sigmoid(z[:, O_GT:O_KR])
    c = _rms(z[:, O_CKV:O_A], kvn_ref[...])
    ckv_ref[...] = c
    kr = z[:, O_KR:O_KRS] * cos[:, :KR_TILE] + z[:, O_KRS:W_IN_COLS] * sin[:, :KR_TILE]
    krope_ref[...] = kr[:, :QK_ROPE]
    kfull_ref[:, 0:KV_LORA] = c.astype(BF16)
    kfull_ref[:, KV_LORA:K_FULL] = kr.astype(BF16)
    cqn = _rms(z[:, O_CQ:O_CKV], qn_ref[...]).astype(BF16)
    q = jnp.dot(cqn, wq_ref[...], preferred_element_type=F32)
    qrope_ref[...] = ((q[:, O_QR:O_QRS] * cos + q[:, O_QRS:W_Q_COLS] * sin)
                      * QK_LOG2_SCALE).astype(BF16)
    for hh in range(N_HEADS):
        qn = q[:, hh * NOPE_PAD:(hh + 1) * NOPE_PAD].astype(BF16)
        qlat_ref[:, hh * KV_LORA:(hh + 1) * KV_LORA] = (jnp.dot(
            qn, wk_ref[hh], preferred_element_type=F32) * QK_LOG2_SCALE).astype(BF16)


def _proj(x, cos, sin, tab_blocks, g, win, qn, wq, kvn, wk):
    n = x.shape[0]
    row = lambda w: pl.BlockSpec((TM, w), lambda i: (i, 0))
    tab = pl.BlockSpec((TM, ROPE_ALL), lambda i: (i % tab_blocks, 0))
    return pl.pallas_call(
        _proj_body,
        grid=(n // TM,),
        in_specs=[row(D_MODEL), tab, tab, _const_spec((1, D_MODEL)),
                  _const_spec((D_MODEL, W_IN_COLS)), _const_spec((1, Q_LORA)),
                  _const_spec((Q_LORA, W_Q_COLS)), _const_spec((1, KV_LORA)),
                  _const_spec((N_HEADS, NOPE_PAD, KV_LORA))],
        out_specs=[row(N_HEADS * KV_LORA), row(ROPE_ALL), row(K_FULL), row(KV_LORA),
                   row(QK_ROPE), row(CONV_DIM)],
        out_shape=[jax.ShapeDtypeStruct((n, N_HEADS * KV_LORA), BF16),
                   jax.ShapeDtypeStruct((n, ROPE_ALL), BF16),
                   jax.ShapeDtypeStruct((n, K_FULL), BF16),
                   jax.ShapeDtypeStruct((n, KV_LORA), F32),
                   jax.ShapeDtypeStruct((n, QK_ROPE), F32),
                   jax.ShapeDtypeStruct((n, CONV_DIM), F32)],
        compiler_params=_params(("parallel",)),
        name="mixer_proj",
    )(x, cos, sin, g, win, qn, wq, kvn, wk)


def _lane_tile(v, n):
    return jnp.concatenate([v] * n, axis=1) if n > 1 else v


def _attn_p_body(qlat_ref, qrope_ref, k_ref, o_ref, qs_ref, m_ref, l_ref, acc_ref):
    i = pl.program_id(1)
    lane = lax.broadcasted_iota(jnp.int32, (BQ, KR_TILE), 1)
    heads_per_tile = KR_TILE // QK_ROPE
    for hh in range(N_HEADS):
        r0 = hh * BQ
        qs_ref[r0:r0 + BQ, 0:KV_LORA] = qlat_ref[:, hh * KV_LORA:(hh + 1) * KV_LORA]
        t = hh // heads_per_tile
        grp = qrope_ref[:, t * KR_TILE:(t + 1) * KR_TILE]
        lo = (hh % heads_per_tile) * QK_ROPE
        keep = jnp.logical_and(lane >= lo, lane < lo + QK_ROPE)
        qs_ref[r0:r0 + BQ, KV_LORA:K_FULL] = jnp.where(keep, grp, jnp.zeros_like(grp))
    m_ref[...] = jnp.full(m_ref.shape, NEG_INF, F32)
    l_ref[...] = jnp.zeros(l_ref.shape, F32)
    acc_ref[...] = jnp.zeros(acc_ref.shape, F32)
    q0 = i * BQ
    n_full = q0 // BK
    n_groups = N_HEADS // ATTN_GROUP_HEADS
    g_rows = ATTN_GROUP_HEADS * BQ

    def part_a(kb, masked):
        start = pl.multiple_of(kb * BK, BK)
        k = k_ref[pl.ds(start, BK), :]
        ps, alphas = [], []
        for g in range(n_groups):
            rs = slice(g * g_rows, (g + 1) * g_rows)
            s = lax.dot_general(qs_ref[rs, :], k, (((1,), (1,)), ((), ())),
                                preferred_element_type=F32)
            if masked:
                r = lax.broadcasted_iota(jnp.int32, (g_rows, BK), 0)
                col = lax.broadcasted_iota(jnp.int32, (g_rows, BK), 1)
                qpos = q0 + jnp.bitwise_and(r, BQ - 1)
                s = jnp.where(start + col <= qpos, s, NEG_INF)
            m_prev = m_ref[rs, :]
            m_next = jnp.maximum(m_prev, jnp.max(s, axis=1, keepdims=True))
            alpha = jnp.exp2(m_prev - m_next)
            p = jnp.exp2(s - _lane_tile(m_next, BK // 128))
            l_ref[rs, :] = alpha * l_ref[rs, :] + jnp.sum(p, axis=1, keepdims=True)
            m_ref[rs, :] = m_next
            ps.append(p.astype(BF16))
            alphas.append(alpha)
        return tuple(ps), tuple(alphas)

    def part_b(kb, ps, alphas):
        start = pl.multiple_of(kb * BK, BK)
        v = k_ref[pl.ds(start, BK), 0:KV_LORA]
        for g in range(n_groups):
            rs = slice(g * g_rows, (g + 1) * g_rows)
            acc_ref[rs, :] = (acc_ref[rs, :] * _lane_tile(alphas[g], KV_LORA // 128)
                              + jnp.dot(ps[g], v, preferred_element_type=F32))

    def trip(kb, carry):
        ps, alphas = part_a(kb, False)
        part_b(jnp.maximum(kb - 1, 0), *carry)
        return ps, alphas

    nothing = (tuple(jnp.zeros((g_rows, BK), BF16) for _ in range(n_groups)),
               tuple(jnp.ones((g_rows, 128), F32) for _ in range(n_groups)))
    lagging = lax.fori_loop(0, n_full, trip, nothing)
    ps, alphas = part_a(n_full, True)
    part_b(jnp.maximum(n_full - 1, 0), *lagging)
    part_b(n_full, ps, alphas)
    inv = 1.0 / l_ref[...]
    for hh in range(N_HEADS):
        r0 = hh * BQ
        o_ref[:, hh * KV_LORA:(hh + 1) * KV_LORA] = (
            acc_ref[r0:r0 + BQ, :] * _lane_tile(inv[r0:r0 + BQ, :], KV_LORA // 128)).astype(BF16)


def _attn_prompt(qlat, qrope, kfull, batch, seq):
    nq = seq // BQ
    rows = N_HEADS * BQ
    return pl.pallas_call(
        _attn_p_body,
        grid=(batch, nq),
        in_specs=[pl.BlockSpec((BQ, N_HEADS * KV_LORA), lambda b, i: (b * nq + i, 0)),
                  pl.BlockSpec((BQ, ROPE_ALL), lambda b, i: (b * nq + i, 0)),
                  pl.BlockSpec((seq, K_FULL), lambda b, i: (b, 0))],
        out_specs=pl.BlockSpec((BQ, N_HEADS * KV_LORA), lambda b, i: (b * nq + i, 0)),
        out_shape=jax.ShapeDtypeStruct((batch * seq, N_HEADS * KV_LORA), BF16),
        scratch_shapes=[pltpu.VMEM((rows, K_FULL), BF16), pltpu.VMEM((rows, 128), F32),
                        pltpu.VMEM((rows, 128), F32), pltpu.VMEM((rows, KV_LORA), F32)],
        compiler_params=_params(("parallel", "arbitrary")),
        name="attn_prompt",
    )(qlat, qrope, kfull)


def _decode_chunk(t, first, last, n_chunks, pt_ref, q_ref, qr_ref, cnew_ref, krnew_ref,
                  cc_hbm, ckr_hbm, o_ref, cbuf, krb, m_ref, l_ref, acc_ref, sem_c, sem_k, layer):
    g_pages = cbuf.shape[1]
    ahead = DECODE_SLOTS - 1
    slot = lax.rem(t, DECODE_SLOTS)
    rows = q_ref.shape[0]
    td = rows // N_HEADS

    def page_copies(chunk, sl, g):
        page = pt_ref[chunk * g_pages + g]
        return (pltpu.make_async_copy(cc_hbm.at[layer, page], cbuf.at[sl, g], sem_c.at[sl]),
                pltpu.make_async_copy(ckr_hbm.at[layer, page], krb.at[sl, g], sem_k.at[sl]))

    def start_chunk(chunk):
        sl = lax.rem(chunk, DECODE_SLOTS)

        def issue(g, carry):
            for cp in page_copies(chunk, sl, g):
                cp.start()
            return carry
        lax.fori_loop(0, g_pages, issue, 0, unroll=8)

    @pl.when(t == 0)
    def _():
        for c in range(ahead):
            start_chunk(c)

    @pl.when(t + ahead < n_chunks)
    def _():
        start_chunk(t + ahead)

    pltpu.make_async_copy(cbuf.at[slot], cbuf.at[slot], sem_c.at[slot]).wait()
    pltpu.make_async_copy(krb.at[slot], krb.at[slot], sem_k.at[slot]).wait()

    if first:
        m_ref[...] = jnp.full(m_ref.shape, NEG_INF, F32)
        l_ref[...] = jnp.zeros(l_ref.shape, F32)
        acc_ref[...] = jnp.zeros(acc_ref.shape, F32)

    q = q_ref[...].astype(F32)
    qr = qr_ref[...].astype(F32)
    sub_pages = g_pages // DECODE_SUBCHUNKS
    sub = sub_pages * PAGE_SIZE
    keys, scores = [], []
    for h in range(DECODE_SUBCHUNKS):
        p0 = h * sub_pages
        ks = cbuf[slot, p0:p0 + sub_pages].reshape(sub, KV_LORA)
        krs = jnp.concatenate([krb[slot, p0 + g] for g in range(sub_pages)], axis=1)
        keys.append(ks)
        scores.append(lax.dot_general(q, ks, (((1,), (1,)), ((), ())), preferred_element_type=F32)
                      + jnp.dot(qr, krs, preferred_element_type=F32))
    m_next = m_ref[...]
    l_new = l_ref[...]
    acc = acc_ref[...]
    for ks, s in zip(keys, scores):
        m_prev = m_next
        m_next = jnp.maximum(m_prev, jnp.max(s, axis=1, keepdims=True))
        alpha = jnp.exp2(m_prev - m_next)
        p = jnp.exp2(s - _lane_tile(m_next, sub // 128))
        l_new = alpha * l_new + jnp.sum(p, axis=1, keepdims=True)
        acc = (acc * _lane_tile(alpha, KV_LORA // 128)
               + jnp.dot(p.astype(BF16).astype(F32), ks, preferred_element_type=F32))
    if not last:
        l_ref[...] = l_new
        m_ref[...] = m_next
        acc_ref[...] = acc
        return

    tok = lax.broadcasted_iota(jnp.int32, (rows, 1), 0) // N_HEADS
    s_new = []
    for t2 in range(td):
        cn = cnew_ref[t2:t2 + 1, :].astype(BF16).astype(F32)
        kn = krnew_ref[t2:t2 + 1, :].astype(BF16).astype(F32)
        st = jnp.sum(q * cn, axis=1, keepdims=True) + jnp.sum(qr * kn, axis=1, keepdims=True)
        s_new.append(jnp.where(t2 <= tok, st, NEG_INF))
    m_cur = s_new[0]
    for st in s_new[1:]:
        m_cur = jnp.maximum(m_cur, st)
    m_fin = jnp.maximum(m_next, m_cur)
    alpha = jnp.exp2(m_next - m_fin)
    l_fin = alpha * l_new
    acc_fin = acc * _lane_tile(alpha, KV_LORA // 128)
    for t2 in range(td):
        pt = jnp.exp2(s_new[t2] - m_fin)
        l_fin = l_fin + pt
        cn = cnew_ref[t2:t2 + 1, :].astype(BF16).astype(F32)
        acc_fin = acc_fin + _lane_tile(pt.astype(BF16).astype(F32), KV_LORA // 128) * cn
    o_ref[...] = (acc_fin * _lane_tile(1.0 / l_fin, KV_LORA // 128)).astype(BF16)


def _ffn_decode_body(pt_ref, x_ref, g_ref, wg_ref, wu_ref, wd_ref, qlat_ref, qr_ref, cnew_ref,
                     krnew_ref, cc_hbm, ckr_hbm, y_ref, o_ref, cbuf, krb, m_ref, l_ref, acc_ref,
                     sem_c, sem_k, *, layer, chunks_per_req):
    reqs = qlat_ref.shape[0]
    chunks = reqs * chunks_per_req
    i = pl.program_id(0)
    n_chunks = pl.num_programs(0) * chunks
    x = x_ref[...]
    h = _rms(x, g_ref[...]).astype(BF16)
    y_ref[...] = x + 0.5 * _swiglu(h, wg_ref, wu_ref, wd_ref)
    for c in range(chunks):
        r, j = divmod(c, chunks_per_req)
        _decode_chunk(i * chunks + c, j == 0, j == chunks_per_req - 1, n_chunks, pt_ref,
                      qlat_ref.at[r], qr_ref.at[r], cnew_ref.at[r], krnew_ref.at[r], cc_hbm,
                      ckr_hbm, o_ref.at[r], cbuf, krb, m_ref, l_ref, acc_ref, sem_c, sem_k, layer)


def _ffn_decode(x, g, wg, wu, wd, page_table, qlat, qrope, cnew, krnew, cache_c, cache_kr_t, layer):
    n = x.shape[0]
    bd, rows, _ = qlat.shape
    td = rows // N_HEADS
    n_pages = page_table.shape[1]
    g_pages = PAGES_PER_STEP
    chunks_per_req = n_pages // g_pages
    n_steps = n // FFN_DECODE_TM
    reqs = bd // n_steps
    assert reqs * n_steps == bd and n_steps * FFN_DECODE_TM == n
    row = pl.BlockSpec((FFN_DECODE_TM, D_MODEL), lambda i, pt: (i, 0))
    per_req = lambda r, w: pl.BlockSpec((reqs, r, w), lambda i, pt: (i, 0, 0))
    const = lambda shape: pl.BlockSpec(shape, lambda i, pt: (0,) * len(shape),
                                       pipeline_mode=pl.Buffered(1))
    hbm = pl.BlockSpec(memory_space=pl.ANY)
    grid_spec = pltpu.PrefetchScalarGridSpec(
        num_scalar_prefetch=1,
        grid=(n_steps,),
        in_specs=[row, const((1, D_MODEL)), const((D_MODEL, D_FF)), const((D_MODEL, D_FF)),
                  const((D_FF, D_MODEL)), per_req(rows, KV_LORA), per_req(rows, QK_ROPE),
                  per_req(td, KV_LORA), per_req(td, QK_ROPE), hbm, hbm],
        out_specs=[row, per_req(rows, KV_LORA)],
        scratch_shapes=[pltpu.VMEM((DECODE_SLOTS, g_pages, PAGE_SIZE, KV_LORA), F32),
                        pltpu.VMEM((DECODE_SLOTS, g_pages, QK_ROPE, PAGE_SIZE), F32),
                        pltpu.VMEM((rows, 128), F32), pltpu.VMEM((rows, 128), F32),
                        pltpu.VMEM((rows, KV_LORA), F32),
                        pltpu.SemaphoreType.DMA((DECODE_SLOTS,)),
                        pltpu.SemaphoreType.DMA((DECODE_SLOTS,))])
    return pl.pallas_call(
        functools.partial(_ffn_decode_body, layer=layer, chunks_per_req=chunks_per_req),
        grid_spec=grid_spec,
        out_shape=[jax.ShapeDtypeStruct((n, D_MODEL), F32),
                   jax.ShapeDtypeStruct((bd, rows, KV_LORA), BF16)],
        compiler_params=_params(("arbitrary",)),
        name="ffn1_attn_decode",
    )(page_table.reshape(-1), x, g, wg, wu, wd, qlat, qrope, cnew, krnew, cache_c, cache_kr_t)


def _ln_silu(y, g, b):
    mu = jnp.mean(y, axis=-1, keepdims=True)
    d = y - mu
    var = jnp.mean(d * d, axis=-1, keepdims=True)
    z = d * lax.rsqrt(var + EPS) * g + b
    return z * jax.nn.sigmoid(z)


def _conv_p_body(prev_ref, cur_ref, w_ref, b_ref, g_ref, beta_ref, o_ref, xb_ref):
    i = pl.program_id(1)
    xb_ref[0:CONV_HALO, :] = jnp.where(i > 0, prev_ref[...], 0.0)
    xb_ref[CONV_HALO:CONV_HALO + CONV_TT, :] = cur_ref[...]
    off = CONV_HALO - (CONV_WIDTH - 1)
    for r in range(CONV_TT // CONV_R):
        acc = jnp.zeros((CONV_R // 8, 8, CONV_DIM), F32)
        for k in range(CONV_WIDTH):
            lo = r * CONV_R + off + k
            xs = xb_ref[lo:lo + CONV_R, :].reshape(CONV_R // 8, 8, CONV_DIM)
            acc = acc + xs * w_ref[k][None]
        y = acc.reshape(CONV_R, CONV_DIM) + b_ref[...]
        o_ref[r * CONV_R:(r + 1) * CONV_R, :] = _ln_silu(y, g_ref[...], beta_ref[...]).astype(BF16)


def _conv_prompt(u, w8, b, g, beta, batch, seq):
    nt = seq // CONV_TT
    halo_per_tile = CONV_TT // CONV_HALO
    halo_per_seq = seq // CONV_HALO
    return pl.pallas_call(
        _conv_p_body,
        grid=(batch, nt),
        in_specs=[pl.BlockSpec((CONV_HALO, CONV_DIM),
                               lambda bb, i: (bb * halo_per_seq + jnp.maximum(i * halo_per_tile - 1, 0), 0)),
                  pl.BlockSpec((CONV_TT, CONV_DIM), lambda bb, i: (bb * nt + i, 0)),
                  _const_spec((CONV_WIDTH, 8, CONV_DIM)), _const_spec((1, CONV_DIM)),
                  _const_spec((1, CONV_DIM)), _const_spec((1, CONV_DIM))],
        out_specs=pl.BlockSpec((CONV_TT, CONV_DIM), lambda bb, i: (bb * nt + i, 0)),
        out_shape=jax.ShapeDtypeStruct((batch * seq, CONV_DIM), BF16),
        scratch_shapes=[pltpu.VMEM((CONV_HALO + CONV_TT, CONV_DIM), F32)],
        compiler_params=_params(("parallel", "arbitrary")),
        name="conv_prompt",
    )(u, u, w8, b, g, beta)


def _conv_s_body(ext_ref, w_ref, b_ref, g_ref, beta_ref, o_ref):
    td, bd, _ = o_ref.shape
    for rg in range(bd // CONV_R):
        for t in range(td):
            acc = jnp.zeros((CONV_R // 8, 8, CONV_DIM), F32)
            for k in range(CONV_WIDTH):
                xs = ext_ref[t + k, rg * CONV_R:(rg + 1) * CONV_R, :].reshape(CONV_R // 8, 8, CONV_DIM)
                acc = acc + xs * w_ref[k][None]
            y = acc.reshape(CONV_R, CONV_DIM) + b_ref[...]
            o_ref[t, rg * CONV_R:(rg + 1) * CONV_R, :] = _ln_silu(
                y, g_ref[...], beta_ref[...]).astype(BF16)


def _conv_sample(ext_t, w8, b, g, beta):
    n_ext, bd, _ = ext_t.shape
    td = n_ext - (CONV_WIDTH - 1)
    return pl.pallas_call(
        _conv_s_body,
        out_shape=jax.ShapeDtypeStruct((td, bd, CONV_DIM), BF16),
        compiler_params=pltpu.CompilerParams(vmem_limit_bytes=VMEM_LIMIT),
        name="conv_sample",
    )(ext_t, w8, b, g, beta)


def _mix_body(x_ref, ol_ref, cv_ref, wv_ref, wout_ref, cg_ref, wcq_ref, x1_ref, qc_ref):
    half = N_HEADS * V_DIM
    mla = jnp.dot(ol_ref[...], wv_ref[...], preferred_element_type=F32).astype(BF16)
    x1 = (x_ref[...]
          + jnp.dot(mla, wout_ref[0:half, :], preferred_element_type=F32)
          + jnp.dot(cv_ref[...], wout_ref[half:, :], preferred_element_type=F32))
    x1_ref[...] = x1
    hq = _rms(x1, cg_ref[...]).astype(BF16)
    qc_ref[...] = (jnp.dot(hq, wcq_ref[...], preferred_element_type=F32) * MEM_SCALE).astype(BF16)


def _mix(x, olat, convo, wvbd, wout, cg, wcq):
    n = x.shape[0]
    row = lambda w: pl.BlockSpec((TM, w), lambda i: (i, 0))
    return pl.pallas_call(
        _mix_body,
        grid=(n // TM,),
        in_specs=[row(D_MODEL), row(N_HEADS * KV_LORA), row(CONV_DIM),
                  _const_spec((N_HEADS * KV_LORA, N_HEADS * V_DIM)),
                  _const_spec((D_MODEL, D_MODEL)), _const_spec((1, D_MODEL)),
                  _const_spec((D_MODEL, D_MODEL))],
        out_specs=[row(D_MODEL), row(D_MODEL)],
        out_shape=[jax.ShapeDtypeStruct((n, D_MODEL), F32),
                   jax.ShapeDtypeStruct((n, D_MODEL), BF16)],
        compiler_params=_params(("parallel",)),
        name="mix_out",
    )(x, olat, convo, wvbd, wout, cg, wcq)


def _memkv_body(mem_ref, g_ref, wk_ref, wv_ref, k_ref, v_ref):
    mn = _rms(mem_ref[...], g_ref[...]).astype(BF16)
    k_ref[...] = jnp.dot(mn, wk_ref[...], preferred_element_type=F32)
    v_ref[...] = jnp.dot(mn, wv_ref[...], preferred_element_type=F32)


def _memkv(mem, g, wk, wv):
    n = mem.shape[0]
    row = pl.BlockSpec((N_MEM, D_MODEL), lambda i: (i, 0))
    return pl.pallas_call(
        _memkv_body,
        grid=(n // N_MEM,),
        in_specs=[row, _const_spec((1, D_MODEL)), _const_spec((D_MODEL, D_MODEL)),
                  _const_spec((D_MODEL, D_MODEL))],
        out_specs=[row, row],
        out_shape=[jax.ShapeDtypeStruct((n, D_MODEL), F32)] * 2,
        compiler_params=_params(("parallel",)),
        name="memory_kv",
    )(mem, g, wk, wv)


def _cross_head(q, k, v):
    s = lax.dot_general(q, k.astype(BF16), (((1,), (1,)), ((), ())), preferred_element_type=F32)
    e = jnp.exp(s - jnp.max(s, axis=1, keepdims=True))
    p = (e * (1.0 / jnp.sum(e, axis=1, keepdims=True))).astype(BF16)
    return jnp.dot(p, v.astype(BF16), preferred_element_type=F32).astype(BF16)


def _cross_p_body(q_ref, k_ref, v_ref, o_ref):
    for hh in range(MEM_HEADS):
        sl = slice(hh * MEM_HEAD_DIM, (hh + 1) * MEM_HEAD_DIM)
        o_ref[:, sl] = _cross_head(q_ref[:, sl], k_ref[:, sl], v_ref[:, sl])


def _cross_prompt(qc, mk, mv, batch, seq):
    nt = seq // TM
    kv = pl.BlockSpec((N_MEM, D_MODEL), lambda b, i: (b, 0))
    row = pl.BlockSpec((TM, D_MODEL), lambda b, i: (b * nt + i, 0))
    return pl.pallas_call(
        _cross_p_body,
        grid=(batch, nt),
        in_specs=[row, kv, kv],
        out_specs=row,
        out_shape=jax.ShapeDtypeStruct((batch * seq, D_MODEL), BF16),
        compiler_params=_params(("parallel", "arbitrary")),
        name="cross_prompt",
    )(qc, mk, mv)


def _cross_s_body(q_ref, k_ref, v_ref, o_ref):
    for i in range(q_ref.shape[0]):
        o_ref[i] = _cross_s_one(q_ref[i], k_ref[i], v_ref[i])


def _cross_s_one(q, k, v):
    kv = k.astype(BF16)
    vv = v.astype(BF16)
    s2 = lax.dot_general(q, kv, (((1,), (1,)), ((), ())), preferred_element_type=F32)
    nr, nc = s2.shape
    half = nr // 2
    r = lax.broadcasted_iota(jnp.int32, (nr, nc), 0)
    c = lax.broadcasted_iota(jnp.int32, (nr, nc), 1)
    same = jnp.logical_and(r // half == (c // MEM_HEADS) % 2,
                           (r // CROSS_TD) % MEM_HEADS == c % MEM_HEADS)
    part = jnp.where(same, s2, 0.0)
    both = part[0:half] + part[half:nr]
    r2 = lax.broadcasted_iota(jnp.int32, (half, nc), 0)
    c2 = lax.broadcasted_iota(jnp.int32, (half, nc), 1)
    cj = (c2 // MEM_HEADS) % 2
    head_ok = c2 % MEM_HEADS == (r2 // CROSS_TD) % MEM_HEADS
    lo = jnp.where(cj == 0, both, 0.0)
    hi = jnp.where(cj == 1, both, 0.0)
    score = both + pltpu.roll(lo, MEM_HEADS, 1) + pltpu.roll(hi, nc - MEM_HEADS, 1)
    sv = jnp.where(head_ok, score, NEG_INF)
    e = jnp.exp(sv - jnp.max(sv, axis=1, keepdims=True))
    p = e * (2.0 / jnp.sum(e, axis=1, keepdims=True))
    pexp = jnp.concatenate([jnp.where(cj == 0, p, 0.0), jnp.where(cj == 1, p, 0.0)],
                           axis=0).astype(BF16)
    return jnp.dot(pexp, vv, preferred_element_type=F32).astype(BF16)


def _cross_sample(qc, mem_k, mem_v, layer):
    bd = mem_k.shape[1]
    td = qc.shape[0] // bd
    assert td == CROSS_TD and MEM_HEAD_DIM == 2 * 128
    rows = 2 * MEM_HEADS * td
    kv_rows = N_MEM * 2 * MEM_HEADS

    def stored_order(x):
        x = x[layer].reshape(bd, N_MEM, MEM_HEADS, 2, 128)
        return jnp.transpose(x, (0, 1, 3, 2, 4)).reshape(bd, kv_rows, 128)

    q2 = jnp.transpose(qc.reshape(bd, td, MEM_HEADS, 2, 128), (0, 3, 2, 1, 4)).reshape(bd, rows, 128)
    kv = pl.BlockSpec((CROSS_BB, kv_rows, 128), lambda b: (b, 0, 0))
    row = pl.BlockSpec((CROSS_BB, rows, 128), lambda b: (b, 0, 0))
    o2 = pl.pallas_call(
        _cross_s_body,
        grid=(bd // CROSS_BB,),
        in_specs=[row, kv, kv],
        out_specs=row,
        out_shape=jax.ShapeDtypeStruct((bd, rows, 128), BF16),
        compiler_params=_params(("parallel",)),
        name="cross_sample",
    )(q2, stored_order(mem_k), stored_order(mem_v))
    o = jnp.transpose(o2.reshape(bd, 2, MEM_HEADS, td, 128), (0, 3, 2, 1, 4))
    return o.reshape(bd * td, D_MODEL)


def _post_body(x1_ref, o_ref, wco_ref, g_ref, wg_ref, wu_ref, wd_ref, fn_ref, y_ref):
    x2 = x1_ref[...] + jnp.dot(o_ref[...], wco_ref[...], preferred_element_type=F32)
    h = _rms(x2, g_ref[...]).astype(BF16)
    x3 = x2 + 0.5 * _swiglu(h, wg_ref, wu_ref, wd_ref)
    y_ref[...] = _rms(x3, fn_ref[...])


def _post(x1, o, wco, g, wg, wu, wd, fn):
    n = x1.shape[0]
    row = pl.BlockSpec((TM, D_MODEL), lambda i: (i, 0))
    return pl.pallas_call(
        _post_body,
        grid=(n // TM,),
        in_specs=[row, row, _const_spec((D_MODEL, D_MODEL)), _const_spec((1, D_MODEL)),
                  _const_spec((D_MODEL, D_FF)), _const_spec((D_MODEL, D_FF)),
                  _const_spec((D_FF, D_MODEL)), _const_spec((1, D_MODEL))],
        out_specs=row,
        out_shape=jax.ShapeDtypeStruct((n, D_MODEL), F32),
        compiler_params=_params(("parallel",)),
        name="cross_out_ffn2",
    )(x1, o, wco, g, wg, wu, wd, fn)


def _rope_tables(pos):
    half = QK_ROPE // 2
    inv = ROPE_BASE ** (-jnp.arange(half, dtype=F32) / half)
    ang = pos.astype(F32)[:, None] * inv[None, :]
    cos = jnp.cos(ang)
    sin = jnp.sin(ang)
    cos_t = jnp.tile(jnp.concatenate([cos, cos], axis=1), (1, N_HEADS))
    sin_t = jnp.tile(jnp.concatenate([-sin, sin], axis=1), (1, N_HEADS))
    return cos_t, sin_t


def _swap_halves(w):
    half = QK_ROPE // 2
    return jnp.concatenate([w[..., half:], w[..., :half]], axis=-1)


def _layer_weights(l, ffn1_norm, ffn1_w_gate, ffn1_w_up, ffn1_w_down, mix_norm, w_in, q_norm,
                   w_q_up, kv_norm, w_k_up, w_v_up, conv_w, conv_b, conv_ln_g, conv_ln_b, w_out,
                   cross_norm, mem_norm, w_cq, w_mk, w_mv, w_co, ffn2_norm, ffn2_w_gate,
                   ffn2_w_up, ffn2_w_down):
    row = lambda v: v[l][None, :]
    wi = w_in[l]
    o1, o2, o3 = Q_LORA, Q_LORA + KV_LORA, Q_LORA + KV_LORA + QK_ROPE
    w_kr = wi[:, o2:o3]
    rep = KR_TILE // QK_ROPE
    win = jnp.concatenate(
        [wi[:, :o1], wi[:, o1:o2], wi[:, o3:o3 + CONV_DIM], wi[:, o3 + CONV_DIM:],
         jnp.tile(w_kr, (1, rep)), jnp.tile(_swap_halves(w_kr), (1, rep))], axis=1).astype(BF16)
    wq3 = w_q_up[l].reshape(Q_LORA, N_HEADS, QK_NOPE + QK_ROPE)
    wq_nope = jnp.pad(wq3[:, :, :QK_NOPE], ((0, 0), (0, 0), (0, NOPE_PAD - QK_NOPE)))
    wq_rope = wq3[:, :, QK_NOPE:]
    wq = jnp.concatenate(
        [wq_nope.reshape(Q_LORA, N_HEADS * NOPE_PAD), wq_rope.reshape(Q_LORA, ROPE_ALL),
         _swap_halves(wq_rope).reshape(Q_LORA, ROPE_ALL)], axis=1).astype(BF16)
    wk = jnp.pad(jnp.transpose(w_k_up[l], (1, 2, 0)),
                 ((0, 0), (0, NOPE_PAD - QK_NOPE), (0, 0))).astype(BF16)
    wv = w_v_up[l]
    eye = jnp.eye(N_HEADS, dtype=wv.dtype)
    wvbd = jnp.einsum('chv,hg->hcgv', wv, eye).reshape(
        N_HEADS * KV_LORA, N_HEADS * V_DIM).astype(BF16)
    w8 = jnp.broadcast_to(conv_w[l][:, None, :], (CONV_WIDTH, 8, CONV_DIM))
    return dict(
        ffn1=(row(ffn1_norm), ffn1_w_gate[l].astype(BF16), ffn1_w_up[l].astype(BF16),
              ffn1_w_down[l].astype(BF16)),
        proj=(row(mix_norm), win, row(q_norm), wq, row(kv_norm), wk),
        conv=(w8, row(conv_b), row(conv_ln_g), row(conv_ln_b)),
        mix=(wvbd, w_out[l].astype(BF16), row(cross_norm), w_cq[l].astype(BF16)),
        mem=(row(mem_norm), w_mk[l].astype(BF16), w_mv[l].astype(BF16)),
        post=(w_co[l].astype(BF16), row(ffn2_norm), ffn2_w_gate[l].astype(BF16),
              ffn2_w_up[l].astype(BF16), ffn2_w_down[l].astype(BF16)),
    )


def kernel(x_prompt, x_sample, mem_prompt, cache_kv_latent, cache_k_rope, state_conv, cache_mem_k, cache_mem_v, page_table, ffn1_norm, ffn1_w_gate, ffn1_w_up, ffn1_w_down, mix_norm, w_in, q_norm, w_q_up, kv_norm, w_k_up, w_v_up, conv_w, conv_b, conv_ln_g, conv_ln_b, w_out, cross_norm, mem_norm, w_cq, w_mk, w_mv, w_co, ffn2_norm, ffn2_w_gate, ffn2_w_up, ffn2_w_down, final_norm):
    batch, seq, _ = x_prompt.shape
    bd, td, _ = x_sample.shape
    depth = ffn1_norm.shape[0]
    assert depth == 1, "the final norm is fused into the last layer's kernel"
    past_len = page_table.shape[1] * PAGE_SIZE
    n_p, n_s = batch * seq, bd * td
    state_w = CONV_WIDTH - 1

    cos_p, sin_p = _rope_tables(jnp.arange(seq, dtype=jnp.int32))
    cos_s, sin_s = _rope_tables(past_len + jnp.arange(td, dtype=jnp.int32))
    cos_s, sin_s = jnp.tile(cos_s, (TM // td, 1)), jnp.tile(sin_s, (TM // td, 1))

    xp = x_prompt.reshape(n_p, D_MODEL)
    xs = x_sample.reshape(n_s, D_MODEL)
    fn = final_norm[None, :]
    outs = {k: [] for k in ("kvl_p", "kr_p", "cs_p", "mk_p", "mv_p", "kvl_s", "kr_s", "cs_s")}
    for l in range(depth):
        w = _layer_weights(l, ffn1_norm, ffn1_w_gate, ffn1_w_up, ffn1_w_down, mix_norm, w_in,
                           q_norm, w_q_up, kv_norm, w_k_up, w_v_up, conv_w, conv_b, conv_ln_g,
                           conv_ln_b, w_out, cross_norm, mem_norm, w_cq, w_mk, w_mv, w_co,
                           ffn2_norm, ffn2_w_gate, ffn2_w_up, ffn2_w_down)
        xs = _ffn(xs, *w["ffn1"])
        qlat_s, qrope_s, _, ckv_s, krope_s, u_s = _proj(xs, cos_s, sin_s, 1, *w["proj"])
        rows = td * N_HEADS
        xp, olat_s = _ffn_decode(
            xp, *w["ffn1"], page_table,
            qlat_s.reshape(bd, rows, KV_LORA), qrope_s.reshape(bd, rows, QK_ROPE),
            ckv_s.reshape(bd, td, KV_LORA), krope_s.reshape(bd, td, QK_ROPE),
            cache_kv_latent, jnp.swapaxes(cache_k_rope, 2, 3), l)

        qlat, qrope, kfull, ckv, krope, u = _proj(xp, cos_p, sin_p, seq // TM, *w["proj"])
        olat = _attn_prompt(qlat, qrope, kfull, batch, seq)
        convo = _conv_prompt(u, *w["conv"], batch, seq)
        xp1, qc_p = _mix(xp, olat, convo, *w["mix"])
        outs["kvl_p"].append(ckv.reshape(batch, seq, KV_LORA))
        outs["kr_p"].append(krope.reshape(batch, seq, QK_ROPE))
        outs["cs_p"].append(u.reshape(batch, seq, CONV_DIM)[:, seq - state_w:])

        ckv, krope, u = ckv_s, krope_s, u_s
        olat = olat_s.reshape(n_s, N_HEADS * KV_LORA)
        u3 = u.reshape(bd, td, CONV_DIM)
        ext = jnp.concatenate([state_conv[l], u3], axis=1)
        convo = _conv_sample(jnp.transpose(ext, (1, 0, 2)), *w["conv"])
        convo = jnp.transpose(convo, (1, 0, 2)).reshape(n_s, CONV_DIM)
        xs1, qc_s = _mix(xs, olat, convo, *w["mix"])
        outs["kvl_s"].append(ckv.reshape(bd, td, KV_LORA))
        outs["kr_s"].append(krope.reshape(bd, td, QK_ROPE))
        outs["cs_s"].append(ext[:, td:])

        mk, mv = _memkv(mem_prompt.reshape(batch * N_MEM, D_MODEL), *w["mem"])
        o_p = _cross_prompt(qc_p, mk, mv, batch, seq)
        o_s = _cross_sample(qc_s, cache_mem_k, cache_mem_v, l)
        outs["mk_p"].append(mk.reshape(batch, N_MEM, MEM_HEADS, MEM_HEAD_DIM))
        outs["mv_p"].append(mv.reshape(batch, N_MEM, MEM_HEADS, MEM_HEAD_DIM))

        xp = _post(xp1, o_p, *w["post"], fn)
        xs = _post(xs1, o_s, *w["post"], fn)

    st = lambda k: jnp.stack(outs[k])
    return (xp.reshape(batch, seq, D_MODEL), xs.reshape(bd, td, D_MODEL),
            st("kvl_p"), st("kr_p"), st("cs_p"), st("mk_p"), st("mv_p"),
            st("kvl_s"), st("kr_s"), st("cs_s"))
```

```python
import functools

import jax
import jax.numpy as jnp
from jax import lax
from jax.experimental import pallas as pl
from jax.experimental.pallas import tpu as pltpu

F32 = jnp.float32
BF16 = jnp.bfloat16

D_MODEL = 1024
N_HEADS = 8
QK_NOPE = 64
QK_ROPE = 32
V_DIM = 64
Q_LORA = 384
KV_LORA = 256
CONV_DIM = 512
CONV_WIDTH = 31
D_FF = 2816
N_MEM = 256
MEM_HEADS = 4
MEM_HEAD_DIM = D_MODEL // MEM_HEADS
PAGE_SIZE = 128
ROPE_BASE = 10000.0
EPS = 1e-6
NEG_INF = -1e30
ATTN_SCALE = (QK_NOPE + QK_ROPE) ** -0.5
MEM_SCALE = MEM_HEAD_DIM ** -0.5
QK_LOG2_SCALE = ATTN_SCALE * 1.4426950408889634

VMEM_LIMIT = 56 * 1024 * 1024

TM = 512
NOPE_PAD = 128
ROPE_ALL = N_HEADS * QK_ROPE
KR_TILE = 128
K_FULL = KV_LORA + KR_TILE
O_CQ, O_CKV, O_A, O_GT, O_KR, O_KRS, W_IN_COLS = 0, 384, 640, 1152, 1664, 1792, 1920
O_QN, O_QR, O_QRS, W_Q_COLS = 0, 1024, 1280, 1536

BQ = 512
BK = 512
ATTN_GROUP_HEADS = 1
PAGES_PER_STEP = 64
FFN_DECODE_TM = 128
DECODE_SLOTS = 3
DECODE_SUBCHUNKS = 4
CROSS_TD = 4
CROSS_BB = 4
CONV_TT = 256
CONV_R = 32
CONV_HALO = 32


def _const_spec(shape):
    nd = len(shape)
    return pl.BlockSpec(shape, lambda *_: (0,) * nd, pipeline_mode=pl.Buffered(1))


def _params(sem):
    return pltpu.CompilerParams(dimension_semantics=sem, vmem_limit_bytes=VMEM_LIMIT)


def _rms(x, g):
    ms = jnp.mean(x * x, axis=-1, keepdims=True)
    return x * lax.rsqrt(ms + EPS) * g


def _swiglu(h, wg_ref, wu_ref, wd_ref):
    gate = jnp.dot(h, wg_ref[...], preferred_element_type=F32)
    up = jnp.dot(h, wu_ref[...], preferred_element_type=F32)
    a = (gate * jax.nn.sigmoid(gate) * up).astype(BF16)
    return jnp.dot(a, wd_ref[...], preferred_element_type=F32)


def _ffn_body(x_ref, g_ref, wg_ref, wu_ref, wd_ref, o_ref):
    x = x_ref[...]
    h = _rms(x, g_ref[...]).astype(BF16)
    o_ref[...] = x + 0.5 * _swiglu(h, wg_ref, wu_ref, wd_ref)


def _ffn(x, g, wg, wu, wd):
    n = x.shape[0]
    row = pl.BlockSpec((TM, D_MODEL), lambda i: (i, 0))
    return pl.pallas_call(
        _ffn_body,
        grid=(n // TM,),
        in_specs=[row, _const_spec((1, D_MODEL)), _const_spec((D_MODEL, D_FF)),
                  _const_spec((D_MODEL, D_FF)), _const_spec((D_FF, D_MODEL))],
        out_specs=row,
        out_shape=jax.ShapeDtypeStruct((n, D_MODEL), F32),
        compiler_params=_params(("parallel",)),
        name="ffn1",
    )(x, g, wg, wu, wd)


def _proj_body(x_ref, cos_ref, sin_ref, g_ref, win_ref, qn_ref, wq_ref, kvn_ref, wk_ref,
               qlat_ref, qrope_ref, kfull_ref, ckv_ref, krope_ref, u_ref):
    h = _rms(x_ref[...], g_ref[...]).astype(BF16)
    z = jnp.dot(h, win_ref[...], preferred_element_type=F32)
    cos = cos_ref[...]
    sin = sin_ref[...]
    u_ref[...] = z[:, O_A:O_GT] * jax.nn.sigmoid(z[:, O_GT:O_KR])
    c = _rms(z[:, O_CKV:O_A], kvn_ref[...])
    ckv_ref[...] = c
    kr = z[:, O_KR:O_KRS] * cos[:, :KR_TILE] + z[:, O_KRS:W_IN_COLS] * sin[:, :KR_TILE]
    krope_ref[...] = kr[:, :QK_ROPE]
    kfull_ref[:, 0:KV_LORA] = c.astype(BF16)
    kfull_ref[:, KV_LORA:K_FULL] = kr.astype(BF16)
    cqn = _rms(z[:, O_CQ:O_CKV], qn_ref[...]).astype(BF16)
    q = jnp.dot(cqn, wq_ref[...], preferred_element_type=F32)
    qrope_ref[...] = ((q[:, O_QR:O_QRS] * cos + q[:, O_QRS:W_Q_COLS] * sin)
                      * QK_LOG2_SCALE).astype(BF16)
    for hh in range(N_HEADS):
        qn = q[:, hh * NOPE_PAD:(hh + 1) * NOPE_PAD].astype(BF16)
        qlat_ref[:, hh * KV_LORA:(hh + 1) * KV_LORA] = (jnp.dot(
            qn, wk_ref[hh], preferred_element_type=F32) * QK_LOG2_SCALE).astype(BF16)


def _proj(x, cos, sin, tab_blocks, g, win, qn, wq, kvn, wk):
    n = x.shape[0]
    row = lambda w: pl.BlockSpec((TM, w), lambda i: (i, 0))
    tab = pl.BlockSpec((TM, ROPE_ALL), lambda i: (i % tab_blocks, 0))
    return pl.pallas_call(
        _proj_body,
        grid=(n // TM,),
        in_specs=[row(D_MODEL), tab, tab, _const_spec((1, D_MODEL)),
                  _const_spec((D_MODEL, W_IN_COLS)), _const_spec((1, Q_LORA)),
                  _const_spec((Q_LORA, W_Q_COLS)), _const_spec((1, KV_LORA)),
                  _const_spec((N_HEADS, NOPE_PAD, KV_LORA))],
        out_specs=[row(N_HEADS * KV_LORA), row(ROPE_ALL), row(K_FULL), row(KV_LORA),
                   row(QK_ROPE), row(CONV_DIM)],
        out_shape=[jax.ShapeDtypeStruct((n, N_HEADS * KV_LORA), BF16),
                   jax.ShapeDtypeStruct((n, ROPE_ALL), BF16),
                   jax.ShapeDtypeStruct((n, K_FULL), BF16),
                   jax.ShapeDtypeStruct((n, KV_LORA), F32),
                   jax.ShapeDtypeStruct((n, QK_ROPE), F32),
                   jax.ShapeDtypeStruct((n, CONV_DIM), F32)],
        compiler_params=_params(("parallel",)),
        name="mixer_proj",
    )(x, cos, sin, g, win, qn, wq, kvn, wk)


def _lane_tile(v, n):
    return jnp.concatenate([v] * n, axis=1) if n > 1 else v


def _attn_p_body(qlat_ref, qrope_ref, k_ref, o_ref, qs_ref, m_ref, l_ref, acc_ref):
    i = pl.program_id(1)
    lane = lax.broadcasted_iota(jnp.int32, (BQ, KR_TILE), 1)
    heads_per_tile = KR_TILE // QK_ROPE
    for hh in range(N_HEADS):
        r0 = hh * BQ
        qs_ref[r0:r0 + BQ, 0:KV_LORA] = qlat_ref[:, hh * KV_LORA:(hh + 1) * KV_LORA]
        t = hh // heads_per_tile
        grp = qrope_ref[:, t * KR_TILE:(t + 1) * KR_TILE]
        lo = (hh % heads_per_tile) * QK_ROPE
        keep = jnp.logical_and(lane >= lo, lane < lo + QK_ROPE)
        qs_ref[r0:r0 + BQ, KV_LORA:K_FULL] = jnp.where(keep, grp, jnp.zeros_like(grp))
    m_ref[...] = jnp.full(m_ref.shape, NEG_INF, F32)
    l_ref[...] = jnp.zeros(l_ref.shape, F32)
    acc_ref[...] = jnp.zeros(acc_ref.shape, F32)
    q0 = i * BQ
    n_full = q0 // BK
    n_groups = N_HEADS // ATTN_GROUP_HEADS
    g_rows = ATTN_GROUP_HEADS * BQ

    def part_a(kb, masked):
        start = pl.multiple_of(kb * BK, BK)
        k = k_ref[pl.ds(start, BK), :]
        ps, alphas = [], []
        for g in range(n_groups):
            rs = slice(g * g_rows, (g + 1) * g_rows)
            s = lax.dot_general(qs_ref[rs, :], k, (((1,), (1,)), ((), ())),
                                preferred_element_type=F32)
            if masked:
                r = lax.broadcasted_iota(jnp.int32, (g_rows, BK), 0)
                col = lax.broadcasted_iota(jnp.int32, (g_rows, BK), 1)
                qpos = q0 + jnp.bitwise_and(r, BQ - 1)
                s = jnp.where(start + col <= qpos, s, NEG_INF)
            m_prev = m_ref[rs, :]
            m_next = jnp.maximum(m_prev, jnp.max(s, axis=1, keepdims=True))
            alpha = jnp.exp2(m_prev - m_next)
            p = jnp.exp2(s - _lane_tile(m_next, BK // 128))
            l_ref[rs, :] = alpha * l_ref[rs, :] + jnp.sum(p, axis=1, keepdims=True)
            m_ref[rs, :] = m_next
            ps.append(p.astype(BF16))
            alphas.append(alpha)
        return tuple(ps), tuple(alphas)

    def part_b(kb, ps, alphas):
        start = pl.multiple_of(kb * BK, BK)
        v = k_ref[pl.ds(start, BK), 0:KV_LORA]
        for g in range(n_groups):
            rs = slice(g * g_rows, (g + 1) * g_rows)
            acc_ref[rs, :] = (acc_ref[rs, :] * _lane_tile(alphas[g], KV_LORA // 128)
                              + jnp.dot(ps[g], v, preferred_element_type=F32))

    def trip(kb, carry):
        ps, alphas = part_a(kb, False)
        part_b(jnp.maximum(kb - 1, 0), *carry)
        return ps, alphas

    nothing = (tuple(jnp.zeros((g_rows, BK), BF16) for _ in range(n_groups)),
               tuple(jnp.ones((g_rows, 128), F32) for _ in range(n_groups)))
    lagging = lax.fori_loop(0, n_full, trip, nothing)
    ps, alphas = part_a(n_full, True)
    part_b(jnp.maximum(n_full - 1, 0), *lagging)
    part_b(n_full, ps, alphas)
    inv = 1.0 / l_ref[...]
    for hh in range(N_HEADS):
        r0 = hh * BQ
        o_ref[:, hh * KV_LORA:(hh + 1) * KV_LORA] = (
            acc_ref[r0:r0 + BQ, :] * _lane_tile(inv[r0:r0 + BQ, :], KV_LORA // 128)).astype(BF16)


def _attn_prompt(qlat, qrope, kfull, batch, seq):
    nq = seq // BQ
    rows = N_HEADS * BQ
    return pl.pallas_call(
        _attn_p_body,
        grid=(batch, nq),
        in_specs=[pl.BlockSpec((BQ, N_HEADS * KV_LORA), lambda b, i: (b * nq + i, 0)),
                  pl.BlockSpec((BQ, ROPE_ALL), lambda b, i: (b * nq + i, 0)),
                  pl.BlockSpec((seq, K_FULL), lambda b, i: (b, 0))],
        out_specs=pl.BlockSpec((BQ, N_HEADS * KV_LORA), lambda b, i: (b * nq + i, 0)),
        out_shape=jax.ShapeDtypeStruct((batch * seq, N_HEADS * KV_LORA), BF16),
        scratch_shapes=[pltpu.VMEM((rows, K_FULL), BF16), pltpu.VMEM((rows, 128), F32),
                        pltpu.VMEM((rows, 128), F32), pltpu.VMEM((rows, KV_LORA), F32)],
        compiler_params=_params(("parallel", "arbitrary")),
        name="attn_prompt",
    )(qlat, qrope, kfull)


def _decode_chunk(t, first, last, n_chunks, pt_ref, q_ref, qr_ref, cnew_ref, krnew_ref,
                  cc_hbm, ckr_hbm, o_ref, cbuf, krb, m_ref, l_ref, acc_ref, sem_c, sem_k, layer,
                  beside=None):
    g_pages = cbuf.shape[1]
    ahead = DECODE_SLOTS - 1
    slot = lax.rem(t, DECODE_SLOTS)
    rows = q_ref.shape[0]
    td = rows // N_HEADS

    def page_copies(chunk, sl, g):
        page = pt_ref[chunk * g_pages + g]
        return (pltpu.make_async_copy(cc_hbm.at[layer, page], cbuf.at[sl, g], sem_c.at[sl]),
                pltpu.make_async_copy(ckr_hbm.at[layer, page], krb.at[sl, g], sem_k.at[sl]))

    def start_chunk(chunk):
        sl = lax.rem(chunk, DECODE_SLOTS)

        def issue(g, carry):
            for cp in page_copies(chunk, sl, g):
                cp.start()
            return carry
        lax.fori_loop(0, g_pages, issue, 0, unroll=8)

    @pl.when(t == 0)
    def _():
        for c in range(ahead):
            start_chunk(c)

    @pl.when(t + ahead < n_chunks)
    def _():
        start_chunk(t + ahead)

    pltpu.make_async_copy(cbuf.at[slot], cbuf.at[slot], sem_c.at[slot]).wait()
    pltpu.make_async_copy(krb.at[slot], krb.at[slot], sem_k.at[slot]).wait()

    if beside is not None:
        beside()
    if first:
        m_ref[...] = jnp.full(m_ref.shape, NEG_INF, F32)
        l_ref[...] = jnp.zeros(l_ref.shape, F32)
        acc_ref[...] = jnp.zeros(acc_ref.shape, F32)

    q = q_ref[...].astype(F32)
    qr = qr_ref[...].astype(F32)
    sub_pages = g_pages // DECODE_SUBCHUNKS
    sub = sub_pages * PAGE_SIZE
    keys, scores = [], []
    for h in range(DECODE_SUBCHUNKS):
        p0 = h * sub_pages
        ks = cbuf[slot, p0:p0 + sub_pages].reshape(sub, KV_LORA)
        krs = jnp.concatenate([krb[slot, p0 + g] for g in range(sub_pages)], axis=1)
        keys.append(ks)
        scores.append(lax.dot_general(q, ks, (((1,), (1,)), ((), ())), preferred_element_type=F32)
                      + jnp.dot(qr, krs, preferred_element_type=F32))
    m_next = m_ref[...]
    l_new = l_ref[...]
    acc = acc_ref[...]
    for ks, s in zip(keys, scores):
        m_prev = m_next
        m_next = jnp.maximum(m_prev, jnp.max(s, axis=1, keepdims=True))
        alpha = jnp.exp2(m_prev - m_next)
        p = jnp.exp2(s - _lane_tile(m_next, sub // 128))
        l_new = alpha * l_new + jnp.sum(p, axis=1, keepdims=True)
        acc = (acc * _lane_tile(alpha, KV_LORA // 128)
               + jnp.dot(p.astype(BF16).astype(F32), ks, preferred_element_type=F32))
    if not last:
        l_ref[...] = l_new
        m_ref[...] = m_next
        acc_ref[...] = acc
        return

    tok = lax.broadcasted_iota(jnp.int32, (rows, 1), 0) // N_HEADS
    s_new = []
    for t2 in range(td):
        cn = cnew_ref[t2:t2 + 1, :].astype(BF16).astype(F32)
        kn = krnew_ref[t2:t2 + 1, :].astype(BF16).astype(F32)
        st = jnp.sum(q * cn, axis=1, keepdims=True) + jnp.sum(qr * kn, axis=1, keepdims=True)
        s_new.append(jnp.where(t2 <= tok, st, NEG_INF))
    m_cur = s_new[0]
    for st in s_new[1:]:
        m_cur = jnp.maximum(m_cur, st)
    m_fin = jnp.maximum(m_next, m_cur)
    alpha = jnp.exp2(m_next - m_fin)
    l_fin = alpha * l_new
    acc_fin = acc * _lane_tile(alpha, KV_LORA // 128)
    for t2 in range(td):
        pt = jnp.exp2(s_new[t2] - m_fin)
        l_fin = l_fin + pt
        cn = cnew_ref[t2:t2 + 1, :].astype(BF16).astype(F32)
        acc_fin = acc_fin + _lane_tile(pt.astype(BF16).astype(F32), KV_LORA // 128) * cn
    o_ref[...] = (acc_fin * _lane_tile(1.0 / l_fin, KV_LORA // 128)).astype(BF16)


def _ffn_decode_body(pt_ref, x_ref, g_ref, wg_ref, wu_ref, wd_ref, qlat_ref, qr_ref, cnew_ref,
                     krnew_ref, cc_hbm, ckr_hbm, y_ref, o_ref, cbuf, krb, m_ref, l_ref, acc_ref,
                     sem_c, sem_k, *, layer, chunks_per_req):
    reqs = qlat_ref.shape[0]
    chunks = reqs * chunks_per_req
    i = pl.program_id(0)
    n_chunks = pl.num_programs(0) * chunks

    def ffn():
        x = x_ref[...]
        h = _rms(x, g_ref[...]).astype(BF16)
        y_ref[...] = x + 0.5 * _swiglu(h, wg_ref, wu_ref, wd_ref)

    for c in range(chunks):
        r, j = divmod(c, chunks_per_req)
        _decode_chunk(i * chunks + c, j == 0, j == chunks_per_req - 1, n_chunks, pt_ref,
                      qlat_ref.at[r], qr_ref.at[r], cnew_ref.at[r], krnew_ref.at[r], cc_hbm,
                      ckr_hbm, o_ref.at[r], cbuf, krb, m_ref, l_ref, acc_ref, sem_c, sem_k, layer,
                      beside=ffn if c == 0 else None)


def _ffn_decode(x, g, wg, wu, wd, page_table, qlat, qrope, cnew, krnew, cache_c, cache_kr_t, layer):
    n = x.shape[0]
    bd, rows, _ = qlat.shape
    td = rows // N_HEADS
    n_pages = page_table.shape[1]
    g_pages = PAGES_PER_STEP
    chunks_per_req = n_pages // g_pages
    n_steps = n // FFN_DECODE_TM
    reqs = bd // n_steps
    assert reqs * n_steps == bd and n_steps * FFN_DECODE_TM == n
    row = pl.BlockSpec((FFN_DECODE_TM, D_MODEL), lambda i, pt: (i, 0))
    per_req = lambda r, w: pl.BlockSpec((reqs, r, w), lambda i, pt: (i, 0, 0))
    const = lambda shape: pl.BlockSpec(shape, lambda i, pt: (0,) * len(shape),
                                       pipeline_mode=pl.Buffered(1))
    hbm = pl.BlockSpec(memory_space=pl.ANY)
    grid_spec = pltpu.PrefetchScalarGridSpec(
        num_scalar_prefetch=1,
        grid=(n_steps,),
        in_specs=[row, const((1, D_MODEL)), const((D_MODEL, D_FF)), const((D_MODEL, D_FF)),
                  const((D_FF, D_MODEL)), per_req(rows, KV_LORA), per_req(rows, QK_ROPE),
                  per_req(td, KV_LORA), per_req(td, QK_ROPE), hbm, hbm],
        out_specs=[row, per_req(rows, KV_LORA)],
        scratch_shapes=[pltpu.VMEM((DECODE_SLOTS, g_pages, PAGE_SIZE, KV_LORA), F32),
                        pltpu.VMEM((DECODE_SLOTS, g_pages, QK_ROPE, PAGE_SIZE), F32),
                        pltpu.VMEM((rows, 128), F32), pltpu.VMEM((rows, 128), F32),
                        pltpu.VMEM((rows, KV_LORA), F32),
                        pltpu.SemaphoreType.DMA((DECODE_SLOTS,)),
                        pltpu.SemaphoreType.DMA((DECODE_SLOTS,))])
    return pl.pallas_call(
        functools.partial(_ffn_decode_body, layer=layer, chunks_per_req=chunks_per_req),
        grid_spec=grid_spec,
        out_shape=[jax.ShapeDtypeStruct((n, D_MODEL), F32),
                   jax.ShapeDtypeStruct((bd, rows, KV_LORA), BF16)],
        compiler_params=_params(("arbitrary",)),
        name="ffn1_attn_decode",
    )(page_table.reshape(-1), x, g, wg, wu, wd, qlat, qrope, cnew, krnew, cache_c, cache_kr_t)


def _ln_silu(y, g, b):
    mu = jnp.mean(y, axis=-1, keepdims=True)
    d = y - mu
    var = jnp.mean(d * d, axis=-1, keepdims=True)
    z = d * lax.rsqrt(var + EPS) * g + b
    return z * jax.nn.sigmoid(z)


def _conv_p_body(prev_ref, cur_ref, w_ref, b_ref, g_ref, beta_ref, o_ref, xb_ref):
    i = pl.program_id(1)
    xb_ref[0:CONV_HALO, :] = jnp.where(i > 0, prev_ref[...], 0.0)
    xb_ref[CONV_HALO:CONV_HALO + CONV_TT, :] = cur_ref[...]
    off = CONV_HALO - (CONV_WIDTH - 1)
    for r in range(CONV_TT // CONV_R):
        acc = jnp.zeros((CONV_R // 8, 8, CONV_DIM), F32)
        for k in range(CONV_WIDTH):
            lo = r * CONV_R + off + k
            xs = xb_ref[lo:lo + CONV_R, :].reshape(CONV_R // 8, 8, CONV_DIM)
            acc = acc + xs * w_ref[k][None]
        y = acc.reshape(CONV_R, CONV_DIM) + b_ref[...]
        o_ref[r * CONV_R:(r + 1) * CONV_R, :] = _ln_silu(y, g_ref[...], beta_ref[...]).astype(BF16)


def _conv_prompt(u, w8, b, g, beta, batch, seq):
    nt = seq // CONV_TT
    halo_per_tile = CONV_TT // CONV_HALO
    halo_per_seq = seq // CONV_HALO
    return pl.pallas_call(
        _conv_p_body,
        grid=(batch, nt),
        in_specs=[pl.BlockSpec((CONV_HALO, CONV_DIM),
                               lambda bb, i: (bb * halo_per_seq + jnp.maximum(i * halo_per_tile - 1, 0), 0)),
                  pl.BlockSpec((CONV_TT, CONV_DIM), lambda bb, i: (bb * nt + i, 0)),
                  _const_spec((CONV_WIDTH, 8, CONV_DIM)), _const_spec((1, CONV_DIM)),
                  _const_spec((1, CONV_DIM)), _const_spec((1, CONV_DIM))],
        out_specs=pl.BlockSpec((CONV_TT, CONV_DIM), lambda bb, i: (bb * nt + i, 0)),
        out_shape=jax.ShapeDtypeStruct((batch * seq, CONV_DIM), BF16),
        scratch_shapes=[pltpu.VMEM((CONV_HALO + CONV_TT, CONV_DIM), F32)],
        compiler_params=_params(("parallel", "arbitrary")),
        name="conv_prompt",
    )(u, u, w8, b, g, beta)


def _conv_s_body(ext_ref, w_ref, b_ref, g_ref, beta_ref, o_ref):
    td, bd, _ = o_ref.shape
    for rg in range(bd // CONV_R):
        for t in range(td):
            acc = jnp.zeros((CONV_R // 8, 8, CONV_DIM), F32)
            for k in range(CONV_WIDTH):
                xs = ext_ref[t + k, rg * CONV_R:(rg + 1) * CONV_R, :].reshape(CONV_R // 8, 8, CONV_DIM)
                acc = acc + xs * w_ref[k][None]
            y = acc.reshape(CONV_R, CONV_DIM) + b_ref[...]
            o_ref[t, rg * CONV_R:(rg + 1) * CONV_R, :] = _ln_silu(
                y, g_ref[...], beta_ref[...]).astype(BF16)


def _conv_sample(ext_t, w8, b, g, beta):
    n_ext, bd, _ = ext_t.shape
    td = n_ext - (CONV_WIDTH - 1)
    return pl.pallas_call(
        _conv_s_body,
        out_shape=jax.ShapeDtypeStruct((td, bd, CONV_DIM), BF16),
        compiler_params=pltpu.CompilerParams(vmem_limit_bytes=VMEM_LIMIT),
        name="conv_sample",
    )(ext_t, w8, b, g, beta)


def _mix_body(x_ref, ol_ref, cv_ref, wv_ref, wout_ref, cg_ref, wcq_ref, x1_ref, qc_ref):
    half = N_HEADS * V_DIM
    mla = jnp.dot(ol_ref[...], wv_ref[...], preferred_element_type=F32).astype(BF16)
    x1 = (x_ref[...]
          + jnp.dot(mla, wout_ref[0:half, :], preferred_element_type=F32)
          + jnp.dot(cv_ref[...], wout_ref[half:, :], preferred_element_type=F32))
    x1_ref[...] = x1
    hq = _rms(x1, cg_ref[...]).astype(BF16)
    qc_ref[...] = (jnp.dot(hq, wcq_ref[...], preferred_element_type=F32) * MEM_SCALE).astype(BF16)


def _mix(x, olat, convo, wvbd, wout, cg, wcq):
    n = x.shape[0]
    row = lambda w: pl.BlockSpec((TM, w), lambda i: (i, 0))
    return pl.pallas_call(
        _mix_body,
        grid=(n // TM,),
        in_specs=[row(D_MODEL), row(N_HEADS * KV_LORA), row(CONV_DIM),
                  _const_spec((N_HEADS * KV_LORA, N_HEADS * V_DIM)),
                  _const_spec((D_MODEL, D_MODEL)), _const_spec((1, D_MODEL)),
                  _const_spec((D_MODEL, D_MODEL))],
        out_specs=[row(D_MODEL), row(D_MODEL)],
        out_shape=[jax.ShapeDtypeStruct((n, D_MODEL), F32),
                   jax.ShapeDtypeStruct((n, D_MODEL), BF16)],
        compiler_params=_params(("parallel",)),
        name="mix_out",
    )(x, olat, convo, wvbd, wout, cg, wcq)


def _memkv_body(mem_ref, g_ref, wk_ref, wv_ref, k_ref, v_ref):
    mn = _rms(mem_ref[...], g_ref[...]).astype(BF16)
    k_ref[...] = jnp.dot(mn, wk_ref[...], preferred_element_type=F32)
    v_ref[...] = jnp.dot(mn, wv_ref[...], preferred_element_type=F32)


def _memkv(mem, g, wk, wv):
    n = mem.shape[0]
    row = pl.BlockSpec((N_MEM, D_MODEL), lambda i: (i, 0))
    return pl.pallas_call(
        _memkv_body,
        grid=(n // N_MEM,),
        in_specs=[row, _const_spec((1, D_MODEL)), _const_spec((D_MODEL, D_MODEL)),
                  _const_spec((D_MODEL, D_MODEL))],
        out_specs=[row, row],
        out_shape=[jax.ShapeDtypeStruct((n, D_MODEL), F32)] * 2,
        compiler_params=_params(("parallel",)),
        name="memory_kv",
    )(mem, g, wk, wv)


def _cross_head(q, k, v):
    s = lax.dot_general(q, k.astype(BF16), (((1,), (1,)), ((), ())), preferred_element_type=F32)
    e = jnp.exp(s - jnp.max(s, axis=1, keepdims=True))
    p = (e * (1.0 / jnp.sum(e, axis=1, keepdims=True))).astype(BF16)
    return jnp.dot(p, v.astype(BF16), preferred_element_type=F32).astype(BF16)


def _cross_p_body(q_ref, k_ref, v_ref, o_ref):
    for hh in range(MEM_HEADS):
        sl = slice(hh * MEM_HEAD_DIM, (hh + 1) * MEM_HEAD_DIM)
        o_ref[:, sl] = _cross_head(q_ref[:, sl], k_ref[:, sl], v_ref[:, sl])


def _cross_prompt(qc, mk, mv, batch, seq):
    nt = seq // TM
    kv = pl.BlockSpec((N_MEM, D_MODEL), lambda b, i: (b, 0))
    row = pl.BlockSpec((TM, D_MODEL), lambda b, i: (b * nt + i, 0))
    return pl.pallas_call(
        _cross_p_body,
        grid=(batch, nt),
        in_specs=[row, kv, kv],
        out_specs=row,
        out_shape=jax.ShapeDtypeStruct((batch * seq, D_MODEL), BF16),
        compiler_params=_params(("parallel", "arbitrary")),
        name="cross_prompt",
    )(qc, mk, mv)


def _cross_s_body(q_ref, k_ref, v_ref, o_ref):
    for i in range(q_ref.shape[0]):
        o_ref[i] = _cross_s_one(q_ref[i], k_ref[i], v_ref[i])


def _cross_s_one(q, k, v):
    kv = k.astype(BF16)
    vv = v.astype(BF16)
    s2 = lax.dot_general(q, kv, (((1,), (1,)), ((), ())), preferred_element_type=F32)
    nr, nc = s2.shape
    half = nr // 2
    r = lax.broadcasted_iota(jnp.int32, (nr, nc), 0)
    c = lax.broadcasted_iota(jnp.int32, (nr, nc), 1)
    same = jnp.logical_and(r // half == (c // MEM_HEADS) % 2,
                           (r // CROSS_TD) % MEM_HEADS == c % MEM_HEADS)
    part = jnp.where(same, s2, 0.0)
    both = part[0:half] + part[half:nr]
    r2 = lax.broadcasted_iota(jnp.int32, (half, nc), 0)
    c2 = lax.broadcasted_iota(jnp.int32, (half, nc), 1)
    cj = (c2 // MEM_HEADS) % 2
    head_ok = c2 % MEM_HEADS == (r2 // CROSS_TD) % MEM_HEADS
    lo = jnp.where(cj == 0, both, 0.0)
    hi = jnp.where(cj == 1, both, 0.0)
    score = both + pltpu.roll(lo, MEM_HEADS, 1) + pltpu.roll(hi, nc - MEM_HEADS, 1)
    sv = jnp.where(head_ok, score, NEG_INF)
    e = jnp.exp(sv - jnp.max(sv, axis=1, keepdims=True))
    p = e * (2.0 / jnp.sum(e, axis=1, keepdims=True))
    pexp = jnp.concatenate([jnp.where(cj == 0, p, 0.0), jnp.where(cj == 1, p, 0.0)],
                           axis=0).astype(BF16)
    return jnp.dot(pexp, vv, preferred_element_type=F32).astype(BF16)


def _cross_sample(qc, mem_k, mem_v, layer):
    bd = mem_k.shape[1]
    td = qc.shape[0] // bd
    assert td == CROSS_TD and MEM_HEAD_DIM == 2 * 128
    rows = 2 * MEM_HEADS * td
    kv_rows = N_MEM * 2 * MEM_HEADS

    def stored_order(x):
        x = x[layer].reshape(bd, N_MEM, MEM_HEADS, 2, 128)
        return jnp.transpose(x, (0, 1, 3, 2, 4)).reshape(bd, kv_rows, 128)

    q2 = jnp.transpose(qc.reshape(bd, td, MEM_HEADS, 2, 128), (0, 3, 2, 1, 4)).reshape(bd, rows, 128)
    kv = pl.BlockSpec((CROSS_BB, kv_rows, 128), lambda b: (b, 0, 0))
    row = pl.BlockSpec((CROSS_BB, rows, 128), lambda b: (b, 0, 0))
    o2 = pl.pallas_call(
        _cross_s_body,
        grid=(bd // CROSS_BB,),
        in_specs=[row, kv, kv],
        out_specs=row,
        out_shape=jax.ShapeDtypeStruct((bd, rows, 128), BF16),
        compiler_params=_params(("parallel",)),
        name="cross_sample",
    )(q2, stored_order(mem_k), stored_order(mem_v))
    o = jnp.transpose(o2.reshape(bd, 2, MEM_HEADS, td, 128), (0, 3, 2, 1, 4))
    return o.reshape(bd * td, D_MODEL)


def _post_body(x1_ref, o_ref, wco_ref, g_ref, wg_ref, wu_ref, wd_ref, fn_ref, y_ref):
    x2 = x1_ref[...] + jnp.dot(o_ref[...], wco_ref[...], preferred_element_type=F32)
    h = _rms(x2, g_ref[...]).astype(BF16)
    x3 = x2 + 0.5 * _swiglu(h, wg_ref, wu_ref, wd_ref)
    y_ref[...] = _rms(x3, fn_ref[...])


def _post(x1, o, wco, g, wg, wu, wd, fn):
    n = x1.shape[0]
    row = pl.BlockSpec((TM, D_MODEL), lambda i: (i, 0))
    return pl.pallas_call(
        _post_body,
        grid=(n // TM,),
        in_specs=[row, row, _const_spec((D_MODEL, D_MODEL)), _const_spec((1, D_MODEL)),
                  _const_spec((D_MODEL, D_FF)), _const_spec((D_MODEL, D_FF)),
                  _const_spec((D_FF, D_MODEL)), _const_spec((1, D_MODEL))],
        out_specs=row,
        out_shape=jax.ShapeDtypeStruct((n, D_MODEL), F32),
        compiler_params=_params(("parallel",)),
        name="cross_out_ffn2",
    )(x1, o, wco, g, wg, wu, wd, fn)


def _rope_tables(pos):
    half = QK_ROPE // 2
    inv = ROPE_BASE ** (-jnp.arange(half, dtype=F32) / half)
    ang = pos.astype(F32)[:, None] * inv[None, :]
    cos = jnp.cos(ang)
    sin = jnp.sin(ang)
    cos_t = jnp.tile(jnp.concatenate([cos, cos], axis=1), (1, N_HEADS))
    sin_t = jnp.tile(jnp.concatenate([-sin, sin], axis=1), (1, N_HEADS))
    return cos_t, sin_t


def _swap_halves(w):
    half = QK_ROPE // 2
    return jnp.concatenate([w[..., half:], w[..., :half]], axis=-1)


def _layer_weights(l, ffn1_norm, ffn1_w_gate, ffn1_w_up, ffn1_w_down, mix_norm, w_in, q_norm,
                   w_q_up, kv_norm, w_k_up, w_v_up, conv_w, conv_b, conv_ln_g, conv_ln_b, w_out,
                   cross_norm, mem_norm, w_cq, w_mk, w_mv, w_co, ffn2_norm, ffn2_w_gate,
                   ffn2_w_up, ffn2_w_down):
    row = lambda v: v[l][None, :]
    wi = w_in[l]
    o1, o2, o3 = Q_LORA, Q_LORA + KV_LORA, Q_LORA + KV_LORA + QK_ROPE
    w_kr = wi[:, o2:o3]
    rep = KR_TILE // QK_ROPE
    win = jnp.concatenate(
        [wi[:, :o1], wi[:, o1:o2], wi[:, o3:o3 + CONV_DIM], wi[:, o3 + CONV_DIM:],
         jnp.tile(w_kr, (1, rep)), jnp.tile(_swap_halves(w_kr), (1, rep))], axis=1).astype(BF16)
    wq3 = w_q_up[l].reshape(Q_LORA, N_HEADS, QK_NOPE + QK_ROPE)
    wq_nope = jnp.pad(wq3[:, :, :QK_NOPE], ((0, 0), (0, 0), (0, NOPE_PAD - QK_NOPE)))
    wq_rope = wq3[:, :, QK_NOPE:]
    wq = jnp.concatenate(
        [wq_nope.reshape(Q_LORA, N_HEADS * NOPE_PAD), wq_rope.reshape(Q_LORA, ROPE_ALL),
         _swap_halves(wq_rope).reshape(Q_LORA, ROPE_ALL)], axis=1).astype(BF16)
    wk = jnp.pad(jnp.transpose(w_k_up[l], (1, 2, 0)),
                 ((0, 0), (0, NOPE_PAD - QK_NOPE), (0, 0))).astype(BF16)
    wv = w_v_up[l]
    eye = jnp.eye(N_HEADS, dtype=wv.dtype)
    wvbd = jnp.einsum('chv,hg->hcgv', wv, eye).reshape(
        N_HEADS * KV_LORA, N_HEADS * V_DIM).astype(BF16)
    w8 = jnp.broadcast_to(conv_w[l][:, None, :], (CONV_WIDTH, 8, CONV_DIM))
    return dict(
        ffn1=(row(ffn1_norm), ffn1_w_gate[l].astype(BF16), ffn1_w_up[l].astype(BF16),
              ffn1_w_down[l].astype(BF16)),
        proj=(row(mix_norm), win, row(q_norm), wq, row(kv_norm), wk),
        conv=(w8, row(conv_b), row(conv_ln_g), row(conv_ln_b)),
        mix=(wvbd, w_out[l].astype(BF16), row(cross_norm), w_cq[l].astype(BF16)),
        mem=(row(mem_norm), w_mk[l].astype(BF16), w_mv[l].astype(BF16)),
        post=(w_co[l].astype(BF16), row(ffn2_norm), ffn2_w_gate[l].astype(BF16),
              ffn2_w_up[l].astype(BF16), ffn2_w_down[l].astype(BF16)),
    )


def kernel(x_prompt, x_sample, mem_prompt, cache_kv_latent, cache_k_rope, state_conv, cache_mem_k, cache_mem_v, page_table, ffn1_norm, ffn1_w_gate, ffn1_w_up, ffn1_w_down, mix_norm, w_in, q_norm, w_q_up, kv_norm, w_k_up, w_v_up, conv_w, conv_b, conv_ln_g, conv_ln_b, w_out, cross_norm, mem_norm, w_cq, w_mk, w_mv, w_co, ffn2_norm, ffn2_w_gate, ffn2_w_up, ffn2_w_down, final_norm):
    batch, seq, _ = x_prompt.shape
    bd, td, _ = x_sample.shape
    depth = ffn1_norm.shape[0]
    assert depth == 1, "the final norm is fused into the last layer's kernel"
    past_len = page_table.shape[1] * PAGE_SIZE
    n_p, n_s = batch * seq, bd * td
    state_w = CONV_WIDTH - 1

    cos_p, sin_p = _rope_tables(jnp.arange(seq, dtype=jnp.int32))
    cos_s, sin_s = _rope_tables(past_len + jnp.arange(td, dtype=jnp.int32))
    cos_s, sin_s = jnp.tile(cos_s, (TM // td, 1)), jnp.tile(sin_s, (TM // td, 1))

    xp = x_prompt.reshape(n_p, D_MODEL)
    xs = x_sample.reshape(n_s, D_MODEL)
    fn = final_norm[None, :]
    outs = {k: [] for k in ("kvl_p", "kr_p", "cs_p", "mk_p", "mv_p", "kvl_s", "kr_s", "cs_s")}
    for l in range(depth):
        w = _layer_weights(l, ffn1_norm, ffn1_w_gate, ffn1_w_up, ffn1_w_down, mix_norm, w_in,
                           q_norm, w_q_up, kv_norm, w_k_up, w_v_up, conv_w, conv_b, conv_ln_g,
                           conv_ln_b, w_out, cross_norm, mem_norm, w_cq, w_mk, w_mv, w_co,
                           ffn2_norm, ffn2_w_gate, ffn2_w_up, ffn2_w_down)
        xs = _ffn(xs, *w["ffn1"])
        qlat_s, qrope_s, _, ckv_s, krope_s, u_s = _proj(xs, cos_s, sin_s, 1, *w["proj"])
        rows = td * N_HEADS
        xp, olat_s = _ffn_decode(
            xp, *w["ffn1"], page_table,
            qlat_s.reshape(bd, rows, KV_LORA), qrope_s.reshape(bd, rows, QK_ROPE),
            ckv_s.reshape(bd, td, KV_LORA), krope_s.reshape(bd, td, QK_ROPE),
            cache_kv_latent, jnp.swapaxes(cache_k_rope, 2, 3), l)

        qlat, qrope, kfull, ckv, krope, u = _proj(xp, cos_p, sin_p, seq // TM, *w["proj"])
        olat = _attn_prompt(qlat, qrope, kfull, batch, seq)
        convo = _conv_prompt(u, *w["conv"], batch, seq)
        xp1, qc_p = _mix(xp, olat, convo, *w["mix"])
        outs["kvl_p"].append(ckv.reshape(batch, seq, KV_LORA))
        outs["kr_p"].append(krope.reshape(batch, seq, QK_ROPE))
        outs["cs_p"].append(u.reshape(batch, seq, CONV_DIM)[:, seq - state_w:])

        ckv, krope, u = ckv_s, krope_s, u_s
        olat = olat_s.reshape(n_s, N_HEADS * KV_LORA)
        u3 = u.reshape(bd, td, CONV_DIM)
        ext = jnp.concatenate([state_conv[l], u3], axis=1)
        convo = _conv_sample(jnp.transpose(ext, (1, 0, 2)), *w["conv"])
        convo = jnp.transpose(convo, (1, 0, 2)).reshape(n_s, CONV_DIM)
        xs1, qc_s = _mix(xs, olat, convo, *w["mix"])
        outs["kvl_s"].append(ckv.reshape(bd, td, KV_LORA))
        outs["kr_s"].append(krope.reshape(bd, td, QK_ROPE))
        outs["cs_s"].append(ext[:, td:])

        mk, mv = _memkv(mem_prompt.reshape(batch * N_MEM, D_MODEL), *w["mem"])
        o_p = _cross_prompt(qc_p, mk, mv, batch, seq)
        o_s = _cross_sample(qc_s, cache_mem_k, cache_mem_v, l)
        outs["mk_p"].append(mk.reshape(batch, N_MEM, MEM_HEADS, MEM_HEAD_DIM))
        outs["mv_p"].append(mv.reshape(batch, N_MEM, MEM_HEADS, MEM_HEAD_DIM))

        xp = _post(xp1, o_p, *w["post"], fn)
        xs = _post(xs1, o_s, *w["post"], fn)

    st = lambda k: jnp.stack(outs[k])
    return (xp.reshape(batch, seq, D_MODEL), xs.reshape(bd, td, D_MODEL),
            st("kvl_p"), st("kr_p"), st("cs_p"), st("mk_p"), st("mv_p"),
            st("kvl_s"), st("kr_s"), st("cs_s"))
```

```python
import functools

import jax
import jax.numpy as jnp
from jax import lax
from jax.experimental import pallas as pl
from jax.experimental.pallas import tpu as pltpu

F32 = jnp.float32
BF16 = jnp.bfloat16

D_MODEL = 1024
N_HEADS = 8
QK_NOPE = 64
QK_ROPE = 32
V_DIM = 64
Q_LORA = 384
KV_LORA = 256
CONV_DIM = 512
CONV_WIDTH = 31
D_FF = 2816
N_MEM = 256
MEM_HEADS = 4
MEM_HEAD_DIM = D_MODEL // MEM_HEADS
PAGE_SIZE = 128
ROPE_BASE = 10000.0
EPS = 1e-6
NEG_INF = -1e30
ATTN_SCALE = (QK_NOPE + QK_ROPE) ** -0.5
MEM_SCALE = MEM_HEAD_DIM ** -0.5
QK_LOG2_SCALE = ATTN_SCALE * 1.4426950408889634

VMEM_LIMIT = 56 * 1024 * 1024

TM = 512
NOPE_PAD = 128
ROPE_ALL = N_HEADS * QK_ROPE
KR_TILE = 128
K_FULL = KV_LORA + KR_TILE
O_CQ, O_CKV, O_A, O_GT, O_KR, O_KRS, W_IN_COLS = 0, 384, 640, 1152, 1664, 1792, 1920
O_QN, O_QR, O_QRS, W_Q_COLS = 0, 1024, 1280, 1536

BQ = 512
BK = 512
ATTN_GROUP_HEADS = 1
PAGES_PER_STEP = 64
FFN_DECODE_TM = 128
DECODE_SLOTS = 3
DECODE_SUBCHUNKS = 4
CROSS_TD = 4
CROSS_BB = 4
CONV_TT = 256
CONV_R = 64
CONV_HALO = 32
CONV_SPAN = -(-(CONV_R + CONV_WIDTH - 1 + 7) // 8) * 8
CONV_XB_ROWS = -(-(CONV_TT - CONV_R + 7 + CONV_SPAN) // 8) * 8


def _const_spec(shape):
    nd = len(shape)
    return pl.BlockSpec(shape, lambda *_: (0,) * nd, pipeline_mode=pl.Buffered(1))


def _params(sem):
    return pltpu.CompilerParams(dimension_semantics=sem, vmem_limit_bytes=VMEM_LIMIT)


def _rms(x, g):
    ms = jnp.mean(x * x, axis=-1, keepdims=True)
    return x * lax.rsqrt(ms + EPS) * g


def _swiglu(h, wg_ref, wu_ref, wd_ref):
    gate = jnp.dot(h, wg_ref[...], preferred_element_type=F32)
    up = jnp.dot(h, wu_ref[...], preferred_element_type=F32)
    a = (gate * jax.nn.sigmoid(gate) * up).astype(BF16)
    return jnp.dot(a, wd_ref[...], preferred_element_type=F32)


def _ffn_body(x_ref, g_ref, wg_ref, wu_ref, wd_ref, o_ref):
    x = x_ref[...]
    h = _rms(x, g_ref[...]).astype(BF16)
    o_ref[...] = x + 0.5 * _swiglu(h, wg_ref, wu_ref, wd_ref)


def _ffn(x, g, wg, wu, wd):
    n = x.shape[0]
    row = pl.BlockSpec((TM, D_MODEL), lambda i: (i, 0))
    return pl.pallas_call(
        _ffn_body,
        grid=(n // TM,),
        in_specs=[row, _const_spec((1, D_MODEL)), _const_spec((D_MODEL, D_FF)),
                  _const_spec((D_MODEL, D_FF)), _const_spec((D_FF, D_MODEL))],
        out_specs=row,
        out_shape=jax.ShapeDtypeStruct((n, D_MODEL), F32),
        compiler_params=_params(("parallel",)),
        name="ffn1",
    )(x, g, wg, wu, wd)


def _proj_body(x_ref, cos_ref, sin_ref, g_ref, win_ref, qn_ref, wq_ref, kvn_ref, wk_ref,
               qlat_ref, qrope_ref, kfull_ref, ckv_ref, krope_ref, u_ref):
    h = _rms(x_ref[...], g_ref[...]).astype(BF16)
    z = jnp.dot(h, win_ref[...], preferred_element_type=F32)
    cos = cos_ref[...]
    sin = sin_ref[...]
    u_ref[...] = z[:, O_A:O_GT] * jax.nn.sigmoid(z[:, O_GT:O_KR])
    c = _rms(z[:, O_CKV:O_A], kvn_ref[...])
    ckv_ref[...] = c
    kr = z[:, O_KR:O_KRS] * cos[:, :KR_TILE] + z[:, O_KRS:W_IN_COLS] * sin[:, :KR_TILE]
    krope_ref[...] = kr[:, :QK_ROPE]
    kfull_ref[:, 0:KV_LORA] = c.astype(BF16)
    kfull_ref[:, KV_LORA:K_FULL] = kr.astype(BF16)
    cqn = _rms(z[:, O_CQ:O_CKV], qn_ref[...]).astype(BF16)
    q = jnp.dot(cqn, wq_ref[...], preferred_element_type=F32)
    qrope_ref[...] = ((q[:, O_QR:O_QRS] * cos + q[:, O_QRS:W_Q_COLS] * sin)
                      * QK_LOG2_SCALE).astype(BF16)
    for hh in range(N_HEADS):
        qn = q[:, hh * NOPE_PAD:(hh + 1) * NOPE_PAD].astype(BF16)
        qlat_ref[:, hh * KV_LORA:(hh + 1) * KV_LORA] = (jnp.dot(
            qn, wk_ref[hh], preferred_element_type=F32) * QK_LOG2_SCALE).astype(BF16)


def _proj(x, cos, sin, tab_blocks, g, win, qn, wq, kvn, wk):
    n = x.shape[0]
    row = lambda w: pl.BlockSpec((TM, w), lambda i: (i, 0))
    tab = pl.BlockSpec((TM, ROPE_ALL), lambda i: (i % tab_blocks, 0))
    return pl.pallas_call(
        _proj_body,
        grid=(n // TM,),
        in_specs=[row(D_MODEL), tab, tab, _const_spec((1, D_MODEL)),
                  _const_spec((D_MODEL, W_IN_COLS)), _const_spec((1, Q_LORA)),
                  _const_spec((Q_LORA, W_Q_COLS)), _const_spec((1, KV_LORA)),
                  _const_spec((N_HEADS, NOPE_PAD, KV_LORA))],
        out_specs=[row(N_HEADS * KV_LORA), row(ROPE_ALL), row(K_FULL), row(KV_LORA),
                   row(QK_ROPE), row(CONV_DIM)],
        out_shape=[jax.ShapeDtypeStruct((n, N_HEADS * KV_LORA), BF16),
                   jax.ShapeDtypeStruct((n, ROPE_ALL), BF16),
                   jax.ShapeDtypeStruct((n, K_FULL), BF16),
                   jax.ShapeDtypeStruct((n, KV_LORA), F32),
                   jax.ShapeDtypeStruct((n, QK_ROPE), F32),
                   jax.ShapeDtypeStruct((n, CONV_DIM), F32)],
        compiler_params=_params(("parallel",)),
        name="mixer_proj",
    )(x, cos, sin, g, win, qn, wq, kvn, wk)


def _lane_tile(v, n):
    return jnp.concatenate([v] * n, axis=1) if n > 1 else v


def _attn_p_body(qlat_ref, qrope_ref, k_ref, o_ref, qs_ref, m_ref, l_ref, acc_ref):
    i = pl.program_id(1)
    lane = lax.broadcasted_iota(jnp.int32, (BQ, KR_TILE), 1)
    heads_per_tile = KR_TILE // QK_ROPE
    for hh in range(N_HEADS):
        r0 = hh * BQ
        qs_ref[r0:r0 + BQ, 0:KV_LORA] = qlat_ref[:, hh * KV_LORA:(hh + 1) * KV_LORA]
        t = hh // heads_per_tile
        grp = qrope_ref[:, t * KR_TILE:(t + 1) * KR_TILE]
        lo = (hh % heads_per_tile) * QK_ROPE
        keep = jnp.logical_and(lane >= lo, lane < lo + QK_ROPE)
        qs_ref[r0:r0 + BQ, KV_LORA:K_FULL] = jnp.where(keep, grp, jnp.zeros_like(grp))
    m_ref[...] = jnp.full(m_ref.shape, NEG_INF, F32)
    l_ref[...] = jnp.zeros(l_ref.shape, F32)
    acc_ref[...] = jnp.zeros(acc_ref.shape, F32)
    q0 = i * BQ
    n_full = q0 // BK
    n_groups = N_HEADS // ATTN_GROUP_HEADS
    g_rows = ATTN_GROUP_HEADS * BQ

    def part_a(kb, masked):
        start = pl.multiple_of(kb * BK, BK)
        k = k_ref[pl.ds(start, BK), :]
        ps, alphas = [], []
        for g in range(n_groups):
            rs = slice(g * g_rows, (g + 1) * g_rows)
            s = lax.dot_general(qs_ref[rs, :], k, (((1,), (1,)), ((), ())),
                                preferred_element_type=F32)
            if masked:
                r = lax.broadcasted_iota(jnp.int32, (g_rows, BK), 0)
                col = lax.broadcasted_iota(jnp.int32, (g_rows, BK), 1)
                qpos = q0 + jnp.bitwise_and(r, BQ - 1)
                s = jnp.where(start + col <= qpos, s, NEG_INF)
            m_prev = m_ref[rs, :]
            m_next = jnp.maximum(m_prev, jnp.max(s, axis=1, keepdims=True))
            alpha = jnp.exp2(m_prev - m_next)
            p = jnp.exp2(s - _lane_tile(m_next, BK // 128))
            l_ref[rs, :] = alpha * l_ref[rs, :] + jnp.sum(p, axis=1, keepdims=True)
            m_ref[rs, :] = m_next
            ps.append(p.astype(BF16))
            alphas.append(alpha)
        return tuple(ps), tuple(alphas)

    def part_b(kb, ps, alphas):
        start = pl.multiple_of(kb * BK, BK)
        v = k_ref[pl.ds(start, BK), 0:KV_LORA]
        for g in range(n_groups):
            rs = slice(g * g_rows, (g + 1) * g_rows)
            acc_ref[rs, :] = (acc_ref[rs, :] * _lane_tile(alphas[g], KV_LORA // 128)
                              + jnp.dot(ps[g], v, preferred_element_type=F32))

    def trip(kb, carry):
        ps, alphas = part_a(kb, False)
        part_b(jnp.maximum(kb - 1, 0), *carry)
        return ps, alphas

    nothing = (tuple(jnp.zeros((g_rows, BK), BF16) for _ in range(n_groups)),
               tuple(jnp.ones((g_rows, 128), F32) for _ in range(n_groups)))
    lagging = lax.fori_loop(0, n_full, trip, nothing)
    ps, alphas = part_a(n_full, True)
    part_b(jnp.maximum(n_full - 1, 0), *lagging)
    part_b(n_full, ps, alphas)
    inv = 1.0 / l_ref[...]
    for hh in range(N_HEADS):
        r0 = hh * BQ
        o_ref[:, hh * KV_LORA:(hh + 1) * KV_LORA] = (
            acc_ref[r0:r0 + BQ, :] * _lane_tile(inv[r0:r0 + BQ, :], KV_LORA // 128)).astype(BF16)


def _attn_prompt(qlat, qrope, kfull, batch, seq):
    nq = seq // BQ
    rows = N_HEADS * BQ
    return pl.pallas_call(
        _attn_p_body,
        grid=(batch, nq),
        in_specs=[pl.BlockSpec((BQ, N_HEADS * KV_LORA), lambda b, i: (b * nq + i, 0)),
                  pl.BlockSpec((BQ, ROPE_ALL), lambda b, i: (b * nq + i, 0)),
                  pl.BlockSpec((seq, K_FULL), lambda b, i: (b, 0))],
        out_specs=pl.BlockSpec((BQ, N_HEADS * KV_LORA), lambda b, i: (b * nq + i, 0)),
        out_shape=jax.ShapeDtypeStruct((batch * seq, N_HEADS * KV_LORA), BF16),
        scratch_shapes=[pltpu.VMEM((rows, K_FULL), BF16), pltpu.VMEM((rows, 128), F32),
                        pltpu.VMEM((rows, 128), F32), pltpu.VMEM((rows, KV_LORA), F32)],
        compiler_params=_params(("parallel", "arbitrary")),
        name="attn_prompt",
    )(qlat, qrope, kfull)


def _decode_chunk(t, first, last, n_chunks, pt_ref, q_ref, qr_ref, cnew_ref, krnew_ref,
                  cc_hbm, ckr_hbm, o_ref, cbuf, krb, m_ref, l_ref, acc_ref, sem_c, sem_k, layer,
                  beside=None):
    g_pages = cbuf.shape[1]
    ahead = DECODE_SLOTS - 1
    slot = lax.rem(t, DECODE_SLOTS)
    rows = q_ref.shape[0]
    td = rows // N_HEADS

    def page_copies(chunk, sl, g):
        page = pt_ref[chunk * g_pages + g]
        return (pltpu.make_async_copy(cc_hbm.at[layer, page], cbuf.at[sl, g], sem_c.at[sl]),
                pltpu.make_async_copy(ckr_hbm.at[layer, page], krb.at[sl, g], sem_k.at[sl]))

    def start_chunk(chunk):
        sl = lax.rem(chunk, DECODE_SLOTS)

        def issue(g, carry):
            for cp in page_copies(chunk, sl, g):
                cp.start()
            return carry
        lax.fori_loop(0, g_pages, issue, 0, unroll=8)

    @pl.when(t == 0)
    def _():
        for c in range(ahead):
            start_chunk(c)

    @pl.when(t + ahead < n_chunks)
    def _():
        start_chunk(t + ahead)

    pltpu.make_async_copy(cbuf.at[slot], cbuf.at[slot], sem_c.at[slot]).wait()
    pltpu.make_async_copy(krb.at[slot], krb.at[slot], sem_k.at[slot]).wait()

    if beside is not None:
        beside()
    if first:
        m_ref[...] = jnp.full(m_ref.shape, NEG_INF, F32)
        l_ref[...] = jnp.zeros(l_ref.shape, F32)
        acc_ref[...] = jnp.zeros(acc_ref.shape, F32)

    q = q_ref[...].astype(F32)
    qr = qr_ref[...].astype(F32)
    sub_pages = g_pages // DECODE_SUBCHUNKS
    sub = sub_pages * PAGE_SIZE
    keys, scores = [], []
    for h in range(DECODE_SUBCHUNKS):
        p0 = h * sub_pages
        ks = cbuf[slot, p0:p0 + sub_pages].reshape(sub, KV_LORA)
        krs = jnp.concatenate([krb[slot, p0 + g] for g in range(sub_pages)], axis=1)
        keys.append(ks)
        scores.append(lax.dot_general(q, ks, (((1,), (1,)), ((), ())), preferred_element_type=F32)
                      + jnp.dot(qr, krs, preferred_element_type=F32))
    m_next = m_ref[...]
    l_new = l_ref[...]
    acc = acc_ref[...]
    for ks, s in zip(keys, scores):
        m_prev = m_next
        m_next = jnp.maximum(m_prev, jnp.max(s, axis=1, keepdims=True))
        alpha = jnp.exp2(m_prev - m_next)
        p = jnp.exp2(s - _lane_tile(m_next, sub // 128))
        l_new = alpha * l_new + jnp.sum(p, axis=1, keepdims=True)
        acc = (acc * _lane_tile(alpha, KV_LORA // 128)
               + jnp.dot(p.astype(BF16).astype(F32), ks, preferred_element_type=F32))
    if not last:
        l_ref[...] = l_new
        m_ref[...] = m_next
        acc_ref[...] = acc
        return

    tok = lax.broadcasted_iota(jnp.int32, (rows, 1), 0) // N_HEADS
    s_new = []
    for t2 in range(td):
        cn = cnew_ref[t2:t2 + 1, :].astype(BF16).astype(F32)
        kn = krnew_ref[t2:t2 + 1, :].astype(BF16).astype(F32)
        st = jnp.sum(q * cn, axis=1, keepdims=True) + jnp.sum(qr * kn, axis=1, keepdims=True)
        s_new.append(jnp.where(t2 <= tok, st, NEG_INF))
    m_cur = s_new[0]
    for st in s_new[1:]:
        m_cur = jnp.maximum(m_cur, st)
    m_fin = jnp.maximum(m_next, m_cur)
    alpha = jnp.exp2(m_next - m_fin)
    l_fin = alpha * l_new
    acc_fin = acc * _lane_tile(alpha, KV_LORA // 128)
    for t2 in range(td):
        pt = jnp.exp2(s_new[t2] - m_fin)
        l_fin = l_fin + pt
        cn = cnew_ref[t2:t2 + 1, :].astype(BF16).astype(F32)
        acc_fin = acc_fin + _lane_tile(pt.astype(BF16).astype(F32), KV_LORA // 128) * cn
    o_ref[...] = (acc_fin * _lane_tile(1.0 / l_fin, KV_LORA // 128)).astype(BF16)


def _ffn_decode_body(pt_ref, x_ref, g_ref, wg_ref, wu_ref, wd_ref, qlat_ref, qr_ref, cnew_ref,
                     krnew_ref, cc_hbm, ckr_hbm, y_ref, o_ref, cbuf, krb, m_ref, l_ref, acc_ref,
                     sem_c, sem_k, *, layer, chunks_per_req):
    reqs = qlat_ref.shape[0]
    chunks = reqs * chunks_per_req
    i = pl.program_id(0)
    n_chunks = pl.num_programs(0) * chunks

    def ffn():
        x = x_ref[...]
        h = _rms(x, g_ref[...]).astype(BF16)
        y_ref[...] = x + 0.5 * _swiglu(h, wg_ref, wu_ref, wd_ref)

    for c in range(chunks):
        r, j = divmod(c, chunks_per_req)
        _decode_chunk(i * chunks + c, j == 0, j == chunks_per_req - 1, n_chunks, pt_ref,
                      qlat_ref.at[r], qr_ref.at[r], cnew_ref.at[r], krnew_ref.at[r], cc_hbm,
                      ckr_hbm, o_ref.at[r], cbuf, krb, m_ref, l_ref, acc_ref, sem_c, sem_k, layer,
                      beside=ffn if c == 0 else None)


def _ffn_decode(x, g, wg, wu, wd, page_table, qlat, qrope, cnew, krnew, cache_c, cache_kr_t, layer):
    n = x.shape[0]
    bd, rows, _ = qlat.shape
    td = rows // N_HEADS
    n_pages = page_table.shape[1]
    g_pages = PAGES_PER_STEP
    chunks_per_req = n_pages // g_pages
    n_steps = n // FFN_DECODE_TM
    reqs = bd // n_steps
    assert reqs * n_steps == bd and n_steps * FFN_DECODE_TM == n
    row = pl.BlockSpec((FFN_DECODE_TM, D_MODEL), lambda i, pt: (i, 0))
    per_req = lambda r, w: pl.BlockSpec((reqs, r, w), lambda i, pt: (i, 0, 0))
    const = lambda shape: pl.BlockSpec(shape, lambda i, pt: (0,) * len(shape),
                                       pipeline_mode=pl.Buffered(1))
    hbm = pl.BlockSpec(memory_space=pl.ANY)
    grid_spec = pltpu.PrefetchScalarGridSpec(
        num_scalar_prefetch=1,
        grid=(n_steps,),
        in_specs=[row, const((1, D_MODEL)), const((D_MODEL, D_FF)), const((D_MODEL, D_FF)),
                  const((D_FF, D_MODEL)), per_req(rows, KV_LORA), per_req(rows, QK_ROPE),
                  per_req(td, KV_LORA), per_req(td, QK_ROPE), hbm, hbm],
        out_specs=[row, per_req(rows, KV_LORA)],
        scratch_shapes=[pltpu.VMEM((DECODE_SLOTS, g_pages, PAGE_SIZE, KV_LORA), F32),
                        pltpu.VMEM((DECODE_SLOTS, g_pages, QK_ROPE, PAGE_SIZE), F32),
                        pltpu.VMEM((rows, 128), F32), pltpu.VMEM((rows, 128), F32),
                        pltpu.VMEM((rows, KV_LORA), F32),
                        pltpu.SemaphoreType.DMA((DECODE_SLOTS,)),
                        pltpu.SemaphoreType.DMA((DECODE_SLOTS,))])
    return pl.pallas_call(
        functools.partial(_ffn_decode_body, layer=layer, chunks_per_req=chunks_per_req),
        grid_spec=grid_spec,
        out_shape=[jax.ShapeDtypeStruct((n, D_MODEL), F32),
                   jax.ShapeDtypeStruct((bd, rows, KV_LORA), BF16)],
        compiler_params=_params(("arbitrary",)),
        name="ffn1_attn_decode",
    )(page_table.reshape(-1), x, g, wg, wu, wd, qlat, qrope, cnew, krnew, cache_c, cache_kr_t)


def _ln_silu(y, g, b):
    mu = jnp.mean(y, axis=-1, keepdims=True)
    d = y - mu
    var = jnp.mean(d * d, axis=-1, keepdims=True)
    z = d * lax.rsqrt(var + EPS) * g + b
    return z * jax.nn.sigmoid(z)


def _conv_p_body(prev_ref, cur_ref, w_ref, b_ref, g_ref, beta_ref, o_ref, xb_ref):
    i = pl.program_id(1)
    xb_ref[0:CONV_HALO, :] = jnp.where(i > 0, prev_ref[...], 0.0)
    xb_ref[CONV_HALO:CONV_HALO + CONV_TT, :] = cur_ref[...]
    xb_ref[CONV_HALO + CONV_TT:, :] = jnp.zeros((CONV_XB_ROWS - CONV_HALO - CONV_TT, CONV_DIM), F32)
    off = CONV_HALO - (CONV_WIDTH - 1)
    for r in range(CONV_TT // CONV_R):
        base = r * CONV_R
        shifted = [xb_ref[base + s:base + s + CONV_SPAN, :].reshape(CONV_SPAN // 8, 8, CONV_DIM)
                   for s in range(8)]
        acc = jnp.zeros((CONV_R // 8, 8, CONV_DIM), F32)
        for k in range(CONV_WIDTH):
            a, s = divmod(off + k, 8)
            acc = acc + shifted[s][a:a + CONV_R // 8] * w_ref[k][None]
        y = acc.reshape(CONV_R, CONV_DIM) + b_ref[...]
        o_ref[r * CONV_R:(r + 1) * CONV_R, :] = _ln_silu(y, g_ref[...], beta_ref[...]).astype(BF16)


def _conv_prompt(u, w8, b, g, beta, batch, seq):
    nt = seq // CONV_TT
    halo_per_tile = CONV_TT // CONV_HALO
    halo_per_seq = seq // CONV_HALO
    return pl.pallas_call(
        _conv_p_body,
        grid=(batch, nt),
        in_specs=[pl.BlockSpec((CONV_HALO, CONV_DIM),
                               lambda bb, i: (bb * halo_per_seq + jnp.maximum(i * halo_per_tile - 1, 0), 0)),
                  pl.BlockSpec((CONV_TT, CONV_DIM), lambda bb, i: (bb * nt + i, 0)),
                  _const_spec((CONV_WIDTH, 8, CONV_DIM)), _const_spec((1, CONV_DIM)),
                  _const_spec((1, CONV_DIM)), _const_spec((1, CONV_DIM))],
        out_specs=pl.BlockSpec((CONV_TT, CONV_DIM), lambda bb, i: (bb * nt + i, 0)),
        out_shape=jax.ShapeDtypeStruct((batch * seq, CONV_DIM), BF16),
        scratch_shapes=[pltpu.VMEM((CONV_XB_ROWS, CONV_DIM), F32)],
        compiler_params=_params(("parallel", "arbitrary")),
        name="conv_prompt",
    )(u, u, w8, b, g, beta)


def _conv_s_body(ext_ref, w_ref, b_ref, g_ref, beta_ref, o_ref):
    td, bd, _ = o_ref.shape
    for rg in range(bd // CONV_R):
        for t in range(td):
            acc = jnp.zeros((CONV_R // 8, 8, CONV_DIM), F32)
            for k in range(CONV_WIDTH):
                xs = ext_ref[t + k, rg * CONV_R:(rg + 1) * CONV_R, :].reshape(CONV_R // 8, 8, CONV_DIM)
                acc = acc + xs * w_ref[k][None]
            y = acc.reshape(CONV_R, CONV_DIM) + b_ref[...]
            o_ref[t, rg * CONV_R:(rg + 1) * CONV_R, :] = _ln_silu(
                y, g_ref[...], beta_ref[...]).astype(BF16)


def _conv_sample(ext_t, w8, b, g, beta):
    n_ext, bd, _ = ext_t.shape
    td = n_ext - (CONV_WIDTH - 1)
    return pl.pallas_call(
        _conv_s_body,
        out_shape=jax.ShapeDtypeStruct((td, bd, CONV_DIM), BF16),
        compiler_params=pltpu.CompilerParams(vmem_limit_bytes=VMEM_LIMIT),
        name="conv_sample",
    )(ext_t, w8, b, g, beta)


def _mix_body(x_ref, ol_ref, cv_ref, wv_ref, wout_ref, cg_ref, wcq_ref, x1_ref, qc_ref):
    half = N_HEADS * V_DIM
    mla = jnp.dot(ol_ref[...], wv_ref[...], preferred_element_type=F32).astype(BF16)
    x1 = (x_ref[...]
          + jnp.dot(mla, wout_ref[0:half, :], preferred_element_type=F32)
          + jnp.dot(cv_ref[...], wout_ref[half:, :], preferred_element_type=F32))
    x1_ref[...] = x1
    hq = _rms(x1, cg_ref[...]).astype(BF16)
    qc_ref[...] = (jnp.dot(hq, wcq_ref[...], preferred_element_type=F32) * MEM_SCALE).astype(BF16)


def _mix(x, olat, convo, wvbd, wout, cg, wcq):
    n = x.shape[0]
    row = lambda w: pl.BlockSpec((TM, w), lambda i: (i, 0))
    return pl.pallas_call(
        _mix_body,
        grid=(n // TM,),
        in_specs=[row(D_MODEL), row(N_HEADS * KV_LORA), row(CONV_DIM),
                  _const_spec((N_HEADS * KV_LORA, N_HEADS * V_DIM)),
                  _const_spec((D_MODEL, D_MODEL)), _const_spec((1, D_MODEL)),
                  _const_spec((D_MODEL, D_MODEL))],
        out_specs=[row(D_MODEL), row(D_MODEL)],
        out_shape=[jax.ShapeDtypeStruct((n, D_MODEL), F32),
                   jax.ShapeDtypeStruct((n, D_MODEL), BF16)],
        compiler_params=_params(("parallel",)),
        name="mix_out",
    )(x, olat, convo, wvbd, wout, cg, wcq)


def _memkv_body(mem_ref, g_ref, wk_ref, wv_ref, k_ref, v_ref):
    mn = _rms(mem_ref[...], g_ref[...]).astype(BF16)
    k_ref[...] = jnp.dot(mn, wk_ref[...], preferred_element_type=F32)
    v_ref[...] = jnp.dot(mn, wv_ref[...], preferred_element_type=F32)


def _memkv(mem, g, wk, wv):
    n = mem.shape[0]
    row = pl.BlockSpec((N_MEM, D_MODEL), lambda i: (i, 0))
    return pl.pallas_call(
        _memkv_body,
        grid=(n // N_MEM,),
        in_specs=[row, _const_spec((1, D_MODEL)), _const_spec((D_MODEL, D_MODEL)),
                  _const_spec((D_MODEL, D_MODEL))],
        out_specs=[row, row],
        out_shape=[jax.ShapeDtypeStruct((n, D_MODEL), F32)] * 2,
        compiler_params=_params(("parallel",)),
        name="memory_kv",
    )(mem, g, wk, wv)


def _cross_head(q, k, v):
    s = lax.dot_general(q, k.astype(BF16), (((1,), (1,)), ((), ())), preferred_element_type=F32)
    e = jnp.exp(s - jnp.max(s, axis=1, keepdims=True))
    p = (e * (1.0 / jnp.sum(e, axis=1, keepdims=True))).astype(BF16)
    return jnp.dot(p, v.astype(BF16), preferred_element_type=F32).astype(BF16)


def _cross_p_body(q_ref, k_ref, v_ref, o_ref):
    for hh in range(MEM_HEADS):
        sl = slice(hh * MEM_HEAD_DIM, (hh + 1) * MEM_HEAD_DIM)
        o_ref[:, sl] = _cross_head(q_ref[:, sl], k_ref[:, sl], v_ref[:, sl])


def _cross_prompt(qc, mk, mv, batch, seq):
    nt = seq // TM
    kv = pl.BlockSpec((N_MEM, D_MODEL), lambda b, i: (b, 0))
    row = pl.BlockSpec((TM, D_MODEL), lambda b, i: (b * nt + i, 0))
    return pl.pallas_call(
        _cross_p_body,
        grid=(batch, nt),
        in_specs=[row, kv, kv],
        out_specs=row,
        out_shape=jax.ShapeDtypeStruct((batch * seq, D_MODEL), BF16),
        compiler_params=_params(("parallel", "arbitrary")),
        name="cross_prompt",
    )(qc, mk, mv)


def _cross_s_body(q_ref, k_ref, v_ref, o_ref):
    for i in range(q_ref.shape[0]):
        o_ref[i] = _cross_s_one(q_ref[i], k_ref[i], v_ref[i])


def _cross_s_one(q, k, v):
    kv = k.astype(BF16)
    vv = v.astype(BF16)
    s2 = lax.dot_general(q, kv, (((1,), (1,)), ((), ())), preferred_element_type=F32)
    nr, nc = s2.shape
    half = nr // 2
    r = lax.broadcasted_iota(jnp.int32, (nr, nc), 0)
    c = lax.broadcasted_iota(jnp.int32, (nr, nc), 1)
    same = jnp.logical_and(r // half == (c // MEM_HEADS) % 2,
                           (r // CROSS_TD) % MEM_HEADS == c % MEM_HEADS)
    part = jnp.where(same, s2, 0.0)
    both = part[0:half] + part[half:nr]
    r2 = lax.broadcasted_iota(jnp.int32, (half, nc), 0)
    c2 = lax.broadcasted_iota(jnp.int32, (half, nc), 1)
    cj = (c2 // MEM_HEADS) % 2
    head_ok = c2 % MEM_HEADS == (r2 // CROSS_TD) % MEM_HEADS
    lo = jnp.where(cj == 0, both, 0.0)
    hi = jnp.where(cj == 1, both, 0.0)
    score = both + pltpu.roll(lo, MEM_HEADS, 1) + pltpu.roll(hi, nc - MEM_HEADS, 1)
    sv = jnp.where(head_ok, score, NEG_INF)
    e = jnp.exp(sv - jnp.max(sv, axis=1, keepdims=True))
    p = e * (2.0 / jnp.sum(e, axis=1, keepdims=True))
    pexp = jnp.concatenate([jnp.where(cj == 0, p, 0.0), jnp.where(cj == 1, p, 0.0)],
                           axis=0).astype(BF16)
    return jnp.dot(pexp, vv, preferred_element_type=F32).astype(BF16)


def _cross_sample(qc, mem_k, mem_v, layer):
    bd = mem_k.shape[1]
    td = qc.shape[0] // bd
    assert td == CROSS_TD and MEM_HEAD_DIM == 2 * 128
    rows = 2 * MEM_HEADS * td
    kv_rows = N_MEM * 2 * MEM_HEADS

    def stored_order(x):
        x = x[layer].reshape(bd, N_MEM, MEM_HEADS, 2, 128)
        return jnp.transpose(x, (0, 1, 3, 2, 4)).reshape(bd, kv_rows, 128)

    q2 = jnp.transpose(qc.reshape(bd, td, MEM_HEADS, 2, 128), (0, 3, 2, 1, 4)).reshape(bd, rows, 128)
    kv = pl.BlockSpec((CROSS_BB, kv_rows, 128), lambda b: (b, 0, 0))
    row = pl.BlockSpec((CROSS_BB, rows, 128), lambda b: (b, 0, 0))
    o2 = pl.pallas_call(
        _cross_s_body,
        grid=(bd // CROSS_BB,),
        in_specs=[row, kv, kv],
        out_specs=row,
        out_shape=jax.ShapeDtypeStruct((bd, rows, 128), BF16),
        compiler_params=_params(("parallel",)),
        name="cross_sample",
    )(q2, stored_order(mem_k), stored_order(mem_v))
    o = jnp.transpose(o2.reshape(bd, 2, MEM_HEADS, td, 128), (0, 3, 2, 1, 4))
    return o.reshape(bd * td, D_MODEL)


def _post_body(x1_ref, o_ref, wco_ref, g_ref, wg_ref, wu_ref, wd_ref, fn_ref, y_ref):
    x2 = x1_ref[...] + jnp.dot(o_ref[...], wco_ref[...], preferred_element_type=F32)
    h = _rms(x2, g_ref[...]).astype(BF16)
    x3 = x2 + 0.5 * _swiglu(h, wg_ref, wu_ref, wd_ref)
    y_ref[...] = _rms(x3, fn_ref[...])


def _post(x1, o, wco, g, wg, wu, wd, fn):
    n = x1.shape[0]
    row = pl.BlockSpec((TM, D_MODEL), lambda i: (i, 0))
    return pl.pallas_call(
        _post_body,
        grid=(n // TM,),
        in_specs=[row, row, _const_spec((D_MODEL, D_MODEL)), _const_spec((1, D_MODEL)),
                  _const_spec((D_MODEL, D_FF)), _const_spec((D_MODEL, D_FF)),
                  _const_spec((D_FF, D_MODEL)), _const_spec((1, D_MODEL))],
        out_specs=row,
        out_shape=jax.ShapeDtypeStruct((n, D_MODEL), F32),
        compiler_params=_params(("parallel",)),
        name="cross_out_ffn2",
    )(x1, o, wco, g, wg, wu, wd, fn)


def _rope_tables(pos):
    half = QK_ROPE // 2
    inv = ROPE_BASE ** (-jnp.arange(half, dtype=F32) / half)
    ang = pos.astype(F32)[:, None] * inv[None, :]
    cos = jnp.cos(ang)
    sin = jnp.sin(ang)
    cos_t = jnp.tile(jnp.concatenate([cos, cos], axis=1), (1, N_HEADS))
    sin_t = jnp.tile(jnp.concatenate([-sin, sin], axis=1), (1, N_HEADS))
    return cos_t, sin_t


def _swap_halves(w):
    half = QK_ROPE // 2
    return jnp.concatenate([w[..., half:], w[..., :half]], axis=-1)


def _layer_weights(l, ffn1_norm, ffn1_w_gate, ffn1_w_up, ffn1_w_down, mix_norm, w_in, q_norm,
                   w_q_up, kv_norm, w_k_up, w_v_up, conv_w, conv_b, conv_ln_g, conv_ln_b, w_out,
                   cross_norm, mem_norm, w_cq, w_mk, w_mv, w_co, ffn2_norm, ffn2_w_gate,
                   ffn2_w_up, ffn2_w_down):
    row = lambda v: v[l][None, :]
    wi = w_in[l]
    o1, o2, o3 = Q_LORA, Q_LORA + KV_LORA, Q_LORA + KV_LORA + QK_ROPE
    w_kr = wi[:, o2:o3]
    rep = KR_TILE // QK_ROPE
    win = jnp.concatenate(
        [wi[:, :o1], wi[:, o1:o2], wi[:, o3:o3 + CONV_DIM], wi[:, o3 + CONV_DIM:],
         jnp.tile(w_kr, (1, rep)), jnp.tile(_swap_halves(w_kr), (1, rep))], axis=1).astype(BF16)
    wq3 = w_q_up[l].reshape(Q_LORA, N_HEADS, QK_NOPE + QK_ROPE)
    wq_nope = jnp.pad(wq3[:, :, :QK_NOPE], ((0, 0), (0, 0), (0, NOPE_PAD - QK_NOPE)))
    wq_rope = wq3[:, :, QK_NOPE:]
    wq = jnp.concatenate(
        [wq_nope.reshape(Q_LORA, N_HEADS * NOPE_PAD), wq_rope.reshape(Q_LORA, ROPE_ALL),
         _swap_halves(wq_rope).reshape(Q_LORA, ROPE_ALL)], axis=1).astype(BF16)
    wk = jnp.pad(jnp.transpose(w_k_up[l], (1, 2, 0)),
                 ((0, 0), (0, NOPE_PAD - QK_NOPE), (0, 0))).astype(BF16)
    wv = w_v_up[l]
    eye = jnp.eye(N_HEADS, dtype=wv.dtype)
    wvbd = jnp.einsum('chv,hg->hcgv', wv, eye).reshape(
        N_HEADS * KV_LORA, N_HEADS * V_DIM).astype(BF16)
    w8 = jnp.broadcast_to(conv_w[l][:, None, :], (CONV_WIDTH, 8, CONV_DIM))
    return dict(
        ffn1=(row(ffn1_norm), ffn1_w_gate[l].astype(BF16), ffn1_w_up[l].astype(BF16),
              ffn1_w_down[l].astype(BF16)),
        proj=(row(mix_norm), win, row(q_norm), wq, row(kv_norm), wk),
        conv=(w8, row(conv_b), row(conv_ln_g), row(conv_ln_b)),
        mix=(wvbd, w_out[l].astype(BF16), row(cross_norm), w_cq[l].astype(BF16)),
        mem=(row(mem_norm), w_mk[l].astype(BF16), w_mv[l].astype(BF16)),
        post=(w_co[l].astype(BF16), row(ffn2_norm), ffn2_w_gate[l].astype(BF16),
              ffn2_w_up[l].astype(BF16), ffn2_w_down[l].astype(BF16)),
    )


def kernel(x_prompt, x_sample, mem_prompt, cache_kv_latent, cache_k_rope, state_conv, cache_mem_k, cache_mem_v, page_table, ffn1_norm, ffn1_w_gate, ffn1_w_up, ffn1_w_down, mix_norm, w_in, q_norm, w_q_up, kv_norm, w_k_up, w_v_up, conv_w, conv_b, conv_ln_g, conv_ln_b, w_out, cross_norm, mem_norm, w_cq, w_mk, w_mv, w_co, ffn2_norm, ffn2_w_gate, ffn2_w_up, ffn2_w_down, final_norm):
    batch, seq, _ = x_prompt.shape
    bd, td, _ = x_sample.shape
    depth = ffn1_norm.shape[0]
    assert depth == 1, "the final norm is fused into the last layer's kernel"
    past_len = page_table.shape[1] * PAGE_SIZE
    n_p, n_s = batch * seq, bd * td
    state_w = CONV_WIDTH - 1

    cos_p, sin_p = _rope_tables(jnp.arange(seq, dtype=jnp.int32))
    cos_s, sin_s = _rope_tables(past_len + jnp.arange(td, dtype=jnp.int32))
    cos_s, sin_s = jnp.tile(cos_s, (TM // td, 1)), jnp.tile(sin_s, (TM // td, 1))

    xp = x_prompt.reshape(n_p, D_MODEL)
    xs = x_sample.reshape(n_s, D_MODEL)
    fn = final_norm[None, :]
    outs = {k: [] for k in ("kvl_p", "kr_p", "cs_p", "mk_p", "mv_p", "kvl_s", "kr_s", "cs_s")}
    for l in range(depth):
        w = _layer_weights(l, ffn1_norm, ffn1_w_gate, ffn1_w_up, ffn1_w_down, mix_norm, w_in,
                           q_norm, w_q_up, kv_norm, w_k_up, w_v_up, conv_w, conv_b, conv_ln_g,
                           conv_ln_b, w_out, cross_norm, mem_norm, w_cq, w_mk, w_mv, w_co,
                           ffn2_norm, ffn2_w_gate, ffn2_w_up, ffn2_w_down)
        xs = _ffn(xs, *w["ffn1"])
        qlat_s, qrope_s, _, ckv_s, krope_s, u_s = _proj(xs, cos_s, sin_s, 1, *w["proj"])
        rows = td * N_HEADS
        xp, olat_s = _ffn_decode(
            xp, *w["ffn1"], page_table,
            qlat_s.reshape(bd, rows, KV_LORA), qrope_s.reshape(bd, rows, QK_ROPE),
            ckv_s.reshape(bd, td, KV_LORA), krope_s.reshape(bd, td, QK_ROPE),
            cache_kv_latent, jnp.swapaxes(cache_k_rope, 2, 3), l)

        qlat, qrope, kfull, ckv, krope, u = _proj(xp, cos_p, sin_p, seq // TM, *w["proj"])
        olat = _attn_prompt(qlat, qrope, kfull, batch, seq)
        convo = _conv_prompt(u, *w["conv"], batch, seq)
        xp1, qc_p = _mix(xp, olat, convo, *w["mix"])
        outs["kvl_p"].append(ckv.reshape(batch, seq, KV_LORA))
        outs["kr_p"].append(krope.reshape(batch, seq, QK_ROPE))
        outs["cs_p"].append(u.reshape(batch, seq, CONV_DIM)[:, seq - state_w:])

        ckv, krope, u = ckv_s, krope_s, u_s
        olat = olat_s.reshape(n_s, N_HEADS * KV_LORA)
        u3 = u.reshape(bd, td, CONV_DIM)
        ext = jnp.concatenate([state_conv[l], u3], axis=1)
        convo = _conv_sample(jnp.transpose(ext, (1, 0, 2)), *w["conv"])
        convo = jnp.transpose(convo, (1, 0, 2)).reshape(n_s, CONV_DIM)
        xs1, qc_s = _mix(xs, olat, convo, *w["mix"])
        outs["kvl_s"].append(ckv.reshape(bd, td, KV_LORA))
        outs["kr_s"].append(krope.reshape(bd, td, QK_ROPE))
        outs["cs_s"].append(ext[:, td:])

        mk, mv = _memkv(mem_prompt.reshape(batch * N_MEM, D_MODEL), *w["mem"])
        o_p = _cross_prompt(qc_p, mk, mv, batch, seq)
        o_s = _cross_sample(qc_s, cache_mem_k, cache_mem_v, l)
        outs["mk_p"].append(mk.reshape(batch, N_MEM, MEM_HEADS, MEM_HEAD_DIM))
        outs["mv_p"].append(mv.reshape(batch, N_MEM, MEM_HEADS, MEM_HEAD_DIM))

        xp = _post(xp1, o_p, *w["post"], fn)
        xs = _post(xs1, o_s, *w["post"], fn)

    st = lambda k: jnp.stack(outs[k])
    return (xp.reshape(batch, seq, D_MODEL), xs.reshape(bd, td, D_MODEL),
            st("kvl_p"), st("kr_p"), st("cs_p"), st("mk_p"), st("mv_p"),
            st("kvl_s"), st("kr_s"), st("cs_s"))
```
